```python
import math
import jax, jax.numpy as jnp
from jax import lax
import numpy as np

D_MODEL = 1024
BATCH = 4
SEQ = 8192
DEPTH = 2

N_META = 16
EPS = 1e-6
NEG_INF = -1e30
N_A_LAYERS = DEPTH // 2
N_B_LAYERS = DEPTH - N_A_LAYERS
SSM_EXPAND = 2
D_INNER = SSM_EXPAND * D_MODEL
SSM_HEADDIM = 64
SSM_HEADS = D_INNER // SSM_HEADDIM
SSM_GROUPS = 8
SSM_HPG = SSM_HEADS // SSM_GROUPS
SSM_STATE = 128
SSM_CONV = 4
SSM_CHUNK = 256
CONV_DIM = D_INNER + 2 * SSM_GROUPS * SSM_STATE
IN_PROJ_DIM = D_INNER + CONV_DIM + SSM_HEADS
DIFF_HEADS = 8
DIFF_HEAD_DIM = D_MODEL // DIFF_HEADS // 2
DIFF_VDIM = 2 * DIFF_HEAD_DIM
Q_BLOCK = 128
REL_BUCKETS = 32
REL_MAX_DIST = 128
FFN_HIDDEN = (((8 * D_MODEL + 2) // 3) + 255) // 256 * 256

kernel_name = "yoco_mamba2_diffattn_hybrid"


def _rmsnorm(x, w):
    xf = x.astype(jnp.float32)
    y = xf * lax.rsqrt(jnp.mean(xf * xf, axis=-1, keepdims=True) + EPS)
    return (y * w.astype(jnp.float32)).astype(x.dtype)


def _swiglu(u, w_gu, w_down):
    g, v = jnp.split(u @ w_gu, 2, axis=-1)
    return (jax.nn.silu(g) * v) @ w_down


def _causal_dwconv(xbc, w, b):
    c = xbc.shape[-1]
    out = lax.conv_general_dilated(
        xbc, w[:, None, :], window_strides=(1,), padding=[(SSM_CONV - 1, 0)],
        dimension_numbers=("NWC", "WIO", "NWC"), feature_group_count=c)
    return out + b


def _ssd_mixer(u, in_w, conv_w, conv_b, dt_bias, a_log, d_skip, norm_w, out_w):
    bsz, L, _ = u.shape
    zxbcdt = u @ in_w
    z = zxbcdt[..., :D_INNER]
    xbc = zxbcdt[..., D_INNER:D_INNER + CONV_DIM]
    dt_raw = zxbcdt[..., D_INNER + CONV_DIM:]
    xbc = jax.nn.silu(_causal_dwconv(xbc, conv_w, conv_b))
    dt = jax.nn.softplus(dt_raw.astype(jnp.float32) + dt_bias.astype(jnp.float32))

    n_real = L - N_META
    pad_l = SSM_CHUNK - N_META
    pad_r = (-n_real) % SSM_CHUNK
    xbc_p = jnp.pad(xbc.astype(jnp.float32), ((0, 0), (pad_l, pad_r), (0, 0)))
    dt_p = jnp.pad(dt, ((0, 0), (pad_l, pad_r), (0, 0)))
    T = L + pad_l + pad_r
    nc = T // SSM_CHUNK
    G, R, P, N, Q = SSM_GROUPS, SSM_HPG, SSM_HEADDIM, SSM_STATE, SSM_CHUNK
    xs = xbc_p[..., :D_INNER].reshape(bsz, nc, Q, G, R, P)
    Bm = xbc_p[..., D_INNER:D_INNER + G * N].reshape(bsz, nc, Q, G, N)
    Cm = xbc_p[..., D_INNER + G * N:].reshape(bsz, nc, Q, G, N)
    dtc = dt_p.reshape(bsz, nc, Q, G, R)
    A = -jnp.exp(a_log.astype(jnp.float32)).reshape(G, R)
    dA = dtc * A
    to_front = lambda t: jnp.moveaxis(t, 1, 0)
    causal = jnp.tril(jnp.ones((Q, Q), dtype=bool))[None, :, :, None, None]

    def chunk_step(h, inp):
        xc, bc, cc, dtk, dak = inp
        acs = jnp.cumsum(dak, axis=1)
        seg = acs[:, :, None] - acs[:, None, :]
        decay = jnp.exp(jnp.where(causal, seg, NEG_INF))
        cb = jnp.einsum("blgn,bsgn->blsg", cc, bc)
        wts = cb[..., None] * decay * dtk[:, None]
        y_diag = jnp.einsum("blsgr,bsgrp->blgrp", wts, xc)
        y_off = jnp.einsum("blgn,bgrpn->blgrp", cc, h) * jnp.exp(acs)[..., None]
        a_last = acs[:, -1]
        wstate = jnp.exp(a_last[:, None] - acs) * dtk
        h_new = h * jnp.exp(a_last)[..., None, None] + jnp.einsum(
            "bsgn,bsgr,bsgrp->bgrpn", bc, wstate, xc)
        return h_new, y_diag + y_off

    h0 = jnp.zeros((bsz, G, R, P, N), jnp.float32)
    _, ys = lax.scan(chunk_step, h0, (to_front(xs), to_front(Bm), to_front(Cm),
                                      to_front(dtc), to_front(dA)))
    y = jnp.moveaxis(ys, 0, 1).reshape(bsz, T, SSM_HEADS, P)[:, pad_l:pad_l + L]
    x_heads = xbc[..., :D_INNER].astype(jnp.float32).reshape(bsz, L, SSM_HEADS, P)
    y = y + x_heads * d_skip.astype(jnp.float32)[:, None]
    y = y.reshape(bsz, L, D_INNER) * jax.nn.silu(z.astype(jnp.float32))
    yg = y.reshape(bsz, L, G, D_INNER // G)
    yg = yg * lax.rsqrt(jnp.mean(yg * yg, axis=-1, keepdims=True) + EPS)
    y = yg.reshape(bsz, L, D_INNER) * norm_w.astype(jnp.float32)
    return y.astype(u.dtype) @ out_w


def _rel_bucket(rel):
    n = jnp.maximum(-rel, 0)
    max_exact = REL_BUCKETS // 2
    nf = jnp.maximum(n, 1).astype(jnp.float32)
    large = max_exact + (jnp.log(nf / max_exact) / math.log(REL_MAX_DIST / max_exact)
                         * (REL_BUCKETS - max_exact)).astype(jnp.int32)
    large = jnp.minimum(large, REL_BUCKETS - 1)
    return jnp.where(n < max_exact, n, large)


def _diff_attention(u, w_q, lam_q1, lam_k1, lam_q2, lam_k2, subln_w, w_o, k, v, rel_table, lambda_init):
    bsz, L, _ = u.shape
    H, dh = DIFF_HEADS, DIFF_HEAD_DIM
    q = (u @ w_q).reshape(bsz, L, H, 2, dh) * (dh ** -0.5)
    f32 = jnp.float32
    lam = (jnp.exp(jnp.sum(lam_q1.astype(f32) * lam_k1.astype(f32)))
           - jnp.exp(jnp.sum(lam_q2.astype(f32) * lam_k2.astype(f32))) + lambda_init)

    pad_l = Q_BLOCK - N_META
    pad_r = (-(L - N_META)) % Q_BLOCK
    padw = lambda t: jnp.pad(t, ((0, 0), (pad_l, pad_r)) + ((0, 0),) * (t.ndim - 2))
    q_p, k_p, v_p = padw(q), padw(k), padw(v)
    Lp = L + pad_l + pad_r
    nb = Lp // Q_BLOCK
    pos = jnp.arange(Lp, dtype=jnp.int32) - pad_l
    kvalid = (pos >= 0) & (pos < L)
    q_blocks = jnp.moveaxis(q_p.reshape(bsz, nb, Q_BLOCK, H, 2, dh), 1, 0)
    qpos_blocks = pos.reshape(nb, Q_BLOCK)

    def one_block(args):
        qb, qpos = args
        logits = jnp.einsum("bqhcd,bkhcd->bhcqk", qb, k_p).astype(f32)
        bias = rel_table.astype(f32)[_rel_bucket(pos[None, :] - qpos[:, None])]
        bias = jnp.transpose(bias, (2, 0, 1))[None, :, None]
        mask = (pos[None, :] <= qpos[:, None]) & kvalid[None, :]
        logits = jnp.where(mask, logits + bias, NEG_INF)
        p = jax.nn.softmax(logits, axis=-1)
        attn = p[:, :, 0] - lam * p[:, :, 1]
        return jnp.einsum("bhqk,bkhe->bqhe", attn, v_p.astype(f32))

    o = lax.map(one_block, (q_blocks, qpos_blocks))
    o = jnp.moveaxis(o, 0, 1).reshape(bsz, Lp, H, DIFF_VDIM)[:, pad_l:pad_l + L]
    o = o * lax.rsqrt(jnp.mean(o * o, axis=-1, keepdims=True) + EPS)
    o = o * subln_w.astype(f32) * (1.0 - lambda_init)
    return o.reshape(bsz, L, D_MODEL).astype(u.dtype) @ w_o


def setup_inputs(seed: int = 0) -> dict:
    key = jax.random.key(seed)
    ks = iter(jax.random.split(key, 32))
    nrm = lambda shape, s: jax.random.normal(next(ks), shape, jnp.float32) * s
    x = nrm((BATCH, SEQ, D_MODEL), 1.0)
    meta_tokens = nrm((N_META, D_MODEL), 1.0)
    norm_w = 1.0 + nrm((DEPTH, 2, D_MODEL), 0.01)
    ssm_in_w = nrm((N_A_LAYERS, D_MODEL, IN_PROJ_DIM), D_MODEL ** -0.5)
    ssm_conv_w = nrm((N_A_LAYERS, SSM_CONV, CONV_DIM), SSM_CONV ** -0.5)
    ssm_conv_b = nrm((N_A_LAYERS, CONV_DIM), 0.02)
    dt0 = jnp.exp(jax.random.uniform(next(ks), (N_A_LAYERS, SSM_HEADS), jnp.float32,
                                     math.log(1e-3), math.log(1e-1)))
    ssm_dt_bias = dt0 + jnp.log(-jnp.expm1(-dt0))
    ssm_a_log = jnp.log(jax.random.uniform(next(ks), (N_A_LAYERS, SSM_HEADS), jnp.float32, 1.0, 16.0))
    ssm_d = 1.0 + nrm((N_A_LAYERS, SSM_HEADS), 0.01)
    ssm_norm_w = 1.0 + nrm((N_A_LAYERS, D_INNER), 0.01)
    ssm_out_w = nrm((N_A_LAYERS, D_INNER, D_MODEL), D_INNER ** -0.5)
    kv_norm_w = 1.0 + nrm((D_MODEL,), 0.01)
    w_kv = nrm((D_MODEL, 2 * D_MODEL), D_MODEL ** -0.5)
    w_q = nrm((N_B_LAYERS, D_MODEL, D_MODEL), D_MODEL ** -0.5)
    lam_q1 = nrm((N_B_LAYERS, DIFF_HEAD_DIM), 0.1)
    lam_k1 = nrm((N_B_LAYERS, DIFF_HEAD_DIM), 0.1)
    lam_q2 = nrm((N_B_LAYERS, DIFF_HEAD_DIM), 0.1)
    lam_k2 = nrm((N_B_LAYERS, DIFF_HEAD_DIM), 0.1)
    subln_w = 1.0 + nrm((N_B_LAYERS, DIFF_VDIM), 0.01)
    w_o = nrm((N_B_LAYERS, D_MODEL, D_MODEL), D_MODEL ** -0.5)
    rel_bias = nrm((REL_BUCKETS, DIFF_HEADS), 0.5)
    ffn_w_gu = nrm((DEPTH, D_MODEL, 2 * FFN_HIDDEN), D_MODEL ** -0.5)
    ffn_w_down = nrm((DEPTH, FFN_HIDDEN, D_MODEL), FFN_HIDDEN ** -0.5)
    final_norm_w = 1.0 + nrm((D_MODEL,), 0.01)
    return {"x": x, "meta_tokens": meta_tokens, "norm_w": norm_w,
            "ssm_in_w": ssm_in_w, "ssm_conv_w": ssm_conv_w, "ssm_conv_b": ssm_conv_b,
            "ssm_dt_bias": ssm_dt_bias, "ssm_a_log": ssm_a_log, "ssm_d": ssm_d,
            "ssm_norm_w": ssm_norm_w, "ssm_out_w": ssm_out_w,
            "kv_norm_w": kv_norm_w, "w_kv": w_kv, "w_q": w_q,
            "lam_q1": lam_q1, "lam_k1": lam_k1, "lam_q2": lam_q2, "lam_k2": lam_k2,
            "subln_w": subln_w, "w_o": w_o, "rel_bias": rel_bias,
            "ffn_w_gu": ffn_w_gu, "ffn_w_down": ffn_w_down, "final_norm_w": final_norm_w}


def reference(x, meta_tokens, norm_w, ssm_in_w, ssm_conv_w, ssm_conv_b, ssm_dt_bias, ssm_a_log,
              ssm_d, ssm_norm_w, ssm_out_w, kv_norm_w, w_kv, w_q, lam_q1, lam_k1, lam_q2, lam_k2,
              subln_w, w_o, rel_bias, ffn_w_gu, ffn_w_down, final_norm_w):
    bsz = x.shape[0]
    meta = jnp.broadcast_to(meta_tokens.astype(x.dtype)[None], (bsz, N_META, D_MODEL))
    h = jnp.concatenate([meta, x], axis=1)
    L = h.shape[1]
    k_shared = None
    v_shared = None
    for layer in range(DEPTH):
        if layer < N_A_LAYERS:
            i = layer
            h = h + _ssd_mixer(_rmsnorm(h, norm_w[layer, 0]), ssm_in_w[i], ssm_conv_w[i],
                               ssm_conv_b[i], ssm_dt_bias[i], ssm_a_log[i], ssm_d[i],
                               ssm_norm_w[i], ssm_out_w[i])
        else:
            if layer == N_A_LAYERS:
                kv = _rmsnorm(h, kv_norm_w) @ w_kv
                k_shared = kv[..., :D_MODEL].reshape(bsz, L, DIFF_HEADS, 2, DIFF_HEAD_DIM)
                v_shared = kv[..., D_MODEL:].reshape(bsz, L, DIFF_HEADS, DIFF_VDIM)
            j = layer - N_A_LAYERS
            lambda_init = 0.8 - 0.6 * math.exp(-0.3 * layer)
            h = h + _diff_attention(_rmsnorm(h, norm_w[layer, 0]), w_q[j], lam_q1[j], lam_k1[j],
                                    lam_q2[j], lam_k2[j], subln_w[j], w_o[j],
                                    k_shared, v_shared, rel_bias, lambda_init)
        h = h + _swiglu(_rmsnorm(h, norm_w[layer, 1]), ffn_w_gu[layer], ffn_w_down[layer])
    h = _rmsnorm(h, final_norm_w)
    return h[:, N_META:]
```

```python
import functools
import math

import numpy as np
import jax
import jax.numpy as jnp
from jax import lax
from jax.experimental import pallas as pl
from jax.experimental.pallas import tpu as pltpu

F32 = jnp.float32
BF16 = jnp.bfloat16
HIGHEST = lax.Precision.HIGHEST

D_MODEL = 1024
N_META = 16
EPS = 1e-6
NEG_INF = -1e30
SSM_HEADDIM = 64
SSM_HEADS = 32
SSM_GROUPS = 8
SSM_HPG = SSM_HEADS // SSM_GROUPS
SSM_STATE = 128
SSM_CONV = 4
D_INNER = SSM_HEADS * SSM_HEADDIM
GROUP_W = D_INNER // SSM_GROUPS
CONV_DIM = D_INNER + 2 * SSM_GROUPS * SSM_STATE
DIFF_HEADS = 8
DIFF_HEAD_DIM = 64
DIFF_VDIM = 2 * DIFF_HEAD_DIM
REL_BUCKETS = 32
REL_MAX_DIST = 128
FFN_HIDDEN = 2816

SUBLANES = 8
LANES = 128
CHUNK = 256
ROW_TILE = 512
TQ = 512
TK = 256
META_KEYS = 128
FFN_TILE = 1408
VMEM_LIMIT = 56 * 1024 * 1024


def _resident(shape):
    nd = len(shape)
    return pl.BlockSpec(shape, lambda *_: (0,) * nd, pipeline_mode=pl.Buffered(1))


def _params(*sem):
    return pltpu.CompilerParams(dimension_semantics=sem, vmem_limit_bytes=VMEM_LIMIT)


def _rmsnorm(x, w):
    return x * lax.rsqrt(jnp.mean(x * x, axis=-1, keepdims=True) + EPS) * w


def _softplus(v):
    return jnp.maximum(v, 0.0) + jnp.log(1.0 + jnp.exp(-jnp.abs(v)))


def _silu(v):
    return v * jax.nn.sigmoid(v)


_NT = (((1,), (1,)), ((), ()))


def _in_proj_kernel(x_ref, nw_ref, w_ref, wdt_ref, wdtT_ref, z_ref, xbc_ref, dt_ref, dtT_ref):
    xn = _rmsnorm(x_ref[...], nw_ref[...])
    xb = xn.astype(BF16)
    for c0 in range(0, D_INNER, 1024):
        z_ref[:, c0:c0 + 1024] = jnp.dot(
            xb, w_ref[:, c0:c0 + 1024], preferred_element_type=F32).astype(BF16)
    for c0 in range(0, CONV_DIM, 1024):
        xbc_ref[:, c0:c0 + 1024] = jnp.dot(
            xb, w_ref[:, D_INNER + c0:D_INNER + c0 + 1024], preferred_element_type=F32).astype(BF16)
    dt_ref[...] = jnp.dot(xn, wdt_ref[...], precision=HIGHEST, preferred_element_type=F32)
    dtT_ref[...] = lax.dot_general(wdtT_ref[...], xn, _NT, precision=HIGHEST,
                                   preferred_element_type=F32)


def _in_proj(h2d, nw, w_zx, w_dt, w_dtT):
    rows = h2d.shape[0]
    grid = (rows // ROW_TILE,)
    return pl.pallas_call(
        _in_proj_kernel,
        grid=grid,
        in_specs=[
            pl.BlockSpec((ROW_TILE, D_MODEL), lambda i: (i, 0)),
            _resident((1, D_MODEL)),
            _resident((D_MODEL, D_INNER + CONV_DIM)),
            _resident((D_MODEL, SSM_HEADS)),
            _resident((SSM_HEADS, D_MODEL)),
        ],
        out_specs=[
            pl.BlockSpec((ROW_TILE, D_INNER), lambda i: (i, 0)),
            pl.BlockSpec((ROW_TILE, CONV_DIM), lambda i: (i, 0)),
            pl.BlockSpec((ROW_TILE, SSM_HEADS), lambda i: (i, 0)),
            pl.BlockSpec((SSM_HEADS, ROW_TILE), lambda i: (0, i)),
        ],
        out_shape=[
            jax.ShapeDtypeStruct((rows, D_INNER), BF16),
            jax.ShapeDtypeStruct((rows, CONV_DIM), BF16),
            jax.ShapeDtypeStruct((rows, SSM_HEADS), F32),
            jax.ShapeDtypeStruct((SSM_HEADS, rows), F32),
        ],
        compiler_params=_params("parallel"),
        name="in_proj",
    )(h2d, nw, w_zx, w_dt, w_dtT)


def _ssd_kernel(z_ref, xbc_ref, dt_ref, dtT_ref, cw_ref, cb_ref, dtb_ref, dtbT_ref, alog_ref,
                alogT_ref, dsk_ref, nw_ref, y_ref, state_ref, xe_ref, xc_ref, *, n_pad):
    Q = CHUNK
    c = pl.program_id(1)

    @pl.when(c == 0)
    def _():
        state_ref[...] = jnp.zeros_like(state_ref)
        xe_ref[0:SUBLANES, :] = jnp.zeros((SUBLANES, CONV_DIM), F32)

    xe_ref[SUBLANES:SUBLANES + Q, :] = xbc_ref[...].astype(F32)

    def conv_cols(i, carry):
        cs = pl.ds(pl.multiple_of(i * 512, 512), 512)
        acc = cb_ref[:, cs] + cw_ref[3:4, cs] * xe_ref[SUBLANES:SUBLANES + Q, cs]
        for k in range(SSM_CONV - 1):
            off = SUBLANES - (SSM_CONV - 1) + k
            acc = acc + cw_ref[k:k + 1, cs] * xe_ref[off:off + Q, cs]
        xc_ref[:, cs] = _silu(acc)
        return carry

    lax.fori_loop(0, CONV_DIM // 512, conv_cols, 0)
    xe_ref[0:SUBLANES, :] = xe_ref[Q:Q + SUBLANES, :]

    first_valid = jnp.where(c == 0, n_pad, 0)
    row = lax.broadcasted_iota(jnp.int32, (Q, 1), 0)
    col = lax.broadcasted_iota(jnp.int32, (1, Q), 1)
    dtc = jnp.where(row >= first_valid, _softplus(dt_ref[...] + dtb_ref[...]), 0.0)
    dtr = jnp.where(col >= first_valid, _softplus(dtT_ref[...] + dtbT_ref[...]), 0.0)
    dac = dtc * (-jnp.exp(alog_ref[...]))
    dar = dtr * (-jnp.exp(alogT_ref[...]))
    ii = lax.broadcasted_iota(jnp.int32, (Q, Q), 0)
    jj = lax.broadcasted_iota(jnp.int32, (Q, Q), 1)
    tri = ii >= jj
    acs_c = jnp.dot(tri.astype(F32), dac, precision=HIGHEST, preferred_element_type=F32)
    acs_r = jnp.dot(dar, (ii <= jj).astype(F32), precision=HIGHEST, preferred_element_type=F32)
    eacs_c = jnp.exp(acs_c)
    lane = lax.broadcasted_iota(jnp.int32, (1, GROUP_W), 1)

    for g in range(SSM_GROUPS):
        b_g = xc_ref[:, D_INNER + g * SSM_STATE:D_INNER + (g + 1) * SSM_STATE]
        c_g = xc_ref[:, D_INNER + (SSM_GROUPS + g) * SSM_STATE:D_INNER + (SSM_GROUPS + g + 1) * SSM_STATE]
        x_g = xc_ref[:, g * GROUP_W:(g + 1) * GROUP_W]
        c_b = c_g.astype(BF16)
        cb = lax.dot_general(c_b, b_g.astype(BF16), _NT, preferred_element_type=F32)
        b_t = b_g.T
        s_old = state_ref[g]
        y_off = jnp.dot(c_b, s_old.astype(BF16), preferred_element_type=F32)
        y_diag = jnp.zeros((Q, GROUP_W), F32)
        s_add = jnp.zeros((SSM_STATE, GROUP_W), F32)
        scale = jnp.zeros((Q, GROUP_W), F32)
        sdec = jnp.zeros((1, GROUP_W), F32)
        for r in range(SSM_HPG):
            h = g * SSM_HPG + r
            ac = acs_c[:, h:h + 1]
            ar = acs_r[h:h + 1, :]
            dt_h = dtr[h:h + 1, :]
            decay = jnp.exp(jnp.where(tri, ac - ar, NEG_INF))
            w = (cb * decay * dt_h).astype(BF16)
            in_head = (lane >= r * SSM_HEADDIM) & (lane < (r + 1) * SSM_HEADDIM)
            x_r = jnp.where(in_head, x_g, 0.0).astype(BF16)
            y_diag = y_diag + jnp.dot(w, x_r, preferred_element_type=F32)
            a_last = ar[:, Q - 1:Q]
            w_state = jnp.exp(a_last - ar) * dt_h
            s_add = s_add + jnp.dot((b_t * w_state).astype(BF16), x_r, preferred_element_type=F32)
            scale = jnp.where(in_head, eacs_c[:, h:h + 1], scale)
            sdec = jnp.where(in_head, jnp.exp(a_last), sdec)
        state_ref[g] = s_old * sdec + s_add
        gs = slice(g * GROUP_W, (g + 1) * GROUP_W)
        y = y_diag + y_off * scale + x_g * dsk_ref[:, gs]
        y = y * _silu(z_ref[:, gs].astype(F32))
        y_ref[:, gs] = _rmsnorm(y, nw_ref[:, gs]).astype(BF16)


def _ssd(z, xbc, dt, dtT, cw, cb, dtb, dtbT, alog, alogT, dsk, nw, *, batch, n_chunks, n_pad):
    rows = z.shape[0]
    blk = lambda b, c: b * n_chunks + (c + n_chunks - 1) % n_chunks
    return pl.pallas_call(
        functools.partial(_ssd_kernel, n_pad=n_pad),
        grid=(batch, n_chunks),
        in_specs=[
            pl.BlockSpec((CHUNK, D_INNER), lambda b, c: (blk(b, c), 0)),
            pl.BlockSpec((CHUNK, CONV_DIM), lambda b, c: (blk(b, c), 0)),
            pl.BlockSpec((CHUNK, SSM_HEADS), lambda b, c: (blk(b, c), 0)),
            pl.BlockSpec((SSM_HEADS, CHUNK), lambda b, c: (0, blk(b, c))),
            _resident((SSM_CONV, CONV_DIM)),
            _resident((1, CONV_DIM)),
            _resident((1, SSM_HEADS)),
            _resident((SSM_HEADS, 1)),
            _resident((1, SSM_HEADS)),
            _resident((SSM_HEADS, 1)),
            _resident((1, D_INNER)),
            _resident((1, D_INNER)),
        ],
        out_specs=pl.BlockSpec((CHUNK, D_INNER), lambda b, c: (blk(b, c), 0)),
        out_shape=jax.ShapeDtypeStruct((rows, D_INNER), BF16),
        scratch_shapes=[
            pltpu.VMEM((SSM_GROUPS, SSM_STATE, GROUP_W), F32),
            pltpu.VMEM((CHUNK + 2 * SUBLANES, CONV_DIM), F32),
            pltpu.VMEM((CHUNK, CONV_DIM), F32),
        ],
        compiler_params=_params("arbitrary", "arbitrary"),
        name="ssd_scan",
    )(z, xbc, dt, dtT, cw, cb, dtb, dtbT, alog, alogT, dsk, nw)


def _mm_res_kernel(a_ref, w_ref, r_ref, o_ref):
    o_ref[...] = r_ref[...] + jnp.dot(a_ref[...], w_ref[...], preferred_element_type=F32)


def _mm_res(a, w, res, *, rows):
    k = a.shape[1]
    return pl.pallas_call(
        _mm_res_kernel,
        grid=(rows // ROW_TILE,),
        in_specs=[
            pl.BlockSpec((ROW_TILE, k), lambda i: (i, 0)),
            _resident((k, D_MODEL)),
            pl.BlockSpec((ROW_TILE, D_MODEL), lambda i: (i, 0)),
        ],
        out_specs=pl.BlockSpec((ROW_TILE, D_MODEL), lambda i: (i, 0)),
        out_shape=jax.ShapeDtypeStruct((rows, D_MODEL), F32),
        compiler_params=_params("parallel"),
        name="proj_residual",
    )(a, w, res)


def _ffn_kernel(h_ref, nw_ref, wg_ref, wv_ref, wd_ref, fnw_ref, o_ref, *, final_norm):
    x = h_ref[...]
    xb = _rmsnorm(x, nw_ref[...]).astype(BF16)
    acc = x
    for t0 in range(0, FFN_HIDDEN, FFN_TILE):
        ts = slice(t0, t0 + FFN_TILE)
        gate = jnp.dot(xb, wg_ref[:, ts], preferred_element_type=F32)
        val = jnp.dot(xb, wv_ref[:, ts], preferred_element_type=F32)
        act = (_silu(gate) * val).astype(BF16)
        acc = acc + jnp.dot(act, wd_ref[ts, :], preferred_element_type=F32)
    if final_norm:
        acc = _rmsnorm(acc, fnw_ref[...])
    o_ref[...] = acc


def _ffn(h, nw, wg, wv, wd, fnw, *, final_norm):
    rows = h.shape[0]
    return pl.pallas_call(
        functools.partial(_ffn_kernel, final_norm=final_norm),
        grid=(rows // ROW_TILE,),
        in_specs=[
            pl.BlockSpec((ROW_TILE, D_MODEL), lambda i: (i, 0)),
            _resident((1, D_MODEL)),
            _resident((D_MODEL, FFN_HIDDEN)),
            _resident((D_MODEL, FFN_HIDDEN)),
            _resident((FFN_HIDDEN, D_MODEL)),
            _resident((1, D_MODEL)),
        ],
        out_specs=pl.BlockSpec((ROW_TILE, D_MODEL), lambda i: (i, 0)),
        out_shape=jax.ShapeDtypeStruct((rows, D_MODEL), F32),
        compiler_params=_params("parallel"),
        name="ffn",
    )(h, nw, wg, wv, wd, fnw)


def _kvq_kernel(h_ref, kvnw_ref, qnw_ref, wk_ref, wvT_ref, wqT_ref, k_ref, vT_ref, qT_ref):
    x = h_ref[...]
    inv = lax.rsqrt(jnp.mean(x * x, axis=-1, keepdims=True) + EPS)
    xkv = (x * inv * kvnw_ref[...]).astype(BF16)
    xq = (x * inv * qnw_ref[...]).astype(BF16)
    k_ref[...] = jnp.dot(xkv, wk_ref[...], preferred_element_type=F32).astype(BF16)
    vT_ref[0] = lax.dot_general(wvT_ref[...], xkv, _NT, preferred_element_type=F32).astype(BF16)
    qT = lax.dot_general(wqT_ref[...], xq, _NT, preferred_element_type=F32)
    qT_ref[0] = (qT * (DIFF_HEAD_DIM ** -0.5)).astype(BF16)


def _kvq(h, kvnw, qnw, wk, wvT, wqT, *, batch, lp):
    tiles = lp // CHUNK
    return pl.pallas_call(
        _kvq_kernel,
        grid=(batch, tiles),
        in_specs=[
            pl.BlockSpec((CHUNK, D_MODEL), lambda b, i: (b * tiles + i, 0)),
            _resident((1, D_MODEL)),
            _resident((1, D_MODEL)),
            _resident((D_MODEL, D_MODEL)),
            _resident((D_MODEL, D_MODEL)),
            _resident((D_MODEL, D_MODEL)),
        ],
        out_specs=[
            pl.BlockSpec((CHUNK, D_MODEL), lambda b, i: (b * tiles + i, 0)),
            pl.BlockSpec((1, D_MODEL, CHUNK), lambda b, i: (b, 0, i)),
            pl.BlockSpec((1, D_MODEL, CHUNK), lambda b, i: (b, 0, i)),
        ],
        out_shape=[
            jax.ShapeDtypeStruct((batch * lp, D_MODEL), BF16),
            jax.ShapeDtypeStruct((batch, D_MODEL, lp), BF16),
            jax.ShapeDtypeStruct((batch, D_MODEL, lp), BF16),
        ],
        compiler_params=_params("parallel", "parallel"),
        name="kvq_proj",
    )(h, kvnw, qnw, wk, wvT, wqT)


def _rel_bucket_np(n):
    n = np.asarray(n)
    max_exact = REL_BUCKETS // 2
    nf = np.maximum(n, 1).astype(np.float32)
    large = max_exact + (np.log(nf / np.float32(max_exact)) / np.float32(math.log(REL_MAX_DIST / max_exact))
                         * np.float32(REL_BUCKETS - max_exact)).astype(np.int32)
    large = np.minimum(large, REL_BUCKETS - 1)
    return np.where(n < max_exact, n, large).astype(np.int32)


def _bucket_tiles():
    i = np.arange(TK)[:, None]
    j = np.arange(TQ)[None, :]
    near = []
    for off in (TK, 0, -TK):
        dist = j - i + off
        near.append(np.where(dist >= 0, _rel_bucket_np(np.maximum(dist, 0)), -1))
    m = np.arange(META_KEYS)[:, None] - (META_KEYS - N_META)
    dist0 = N_META + j - m
    meta0 = np.where(m >= 0, _rel_bucket_np(np.maximum(dist0, 0)), -1)
    meta_far = np.where(m >= 0, REL_BUCKETS - 1, -1) + 0 * j
    return (np.stack(near).astype(np.int32), np.stack([meta0, meta_far]).astype(np.int32))


def _bias_kernel(tab_ref, near_id_ref, meta_id_ref, near_ref, meta_ref):
    h = pl.program_id(0)
    far = tab_ref[REL_BUCKETS - 1, h]

    def build(ids):
        out = jnp.where(ids < 0, NEG_INF, 0.0).astype(F32)
        for b in range(REL_BUCKETS - 1):
            out = jnp.where(ids == b, tab_ref[b, h] - far, out)
        return out

    near_ref[0] = build(near_id_ref[...])
    meta_ref[0] = build(meta_id_ref[...])


def _bias_tiles(rel_bias):
    near_ids, meta_ids = _bucket_tiles()
    return pl.pallas_call(
        _bias_kernel,
        grid=(DIFF_HEADS,),
        in_specs=[
            pl.BlockSpec(memory_space=pltpu.SMEM),
            _resident((3, TK, TQ)),
            _resident((2, META_KEYS, TQ)),
        ],
        out_specs=[
            pl.BlockSpec((1, 3, TK, TQ), lambda h: (h, 0, 0, 0)),
            pl.BlockSpec((1, 2, META_KEYS, TQ), lambda h: (h, 0, 0, 0)),
        ],
        out_shape=[
            jax.ShapeDtypeStruct((DIFF_HEADS, 3, TK, TQ), F32),
            jax.ShapeDtypeStruct((DIFF_HEADS, 2, META_KEYS, TQ), F32),
        ],
        compiler_params=_params("arbitrary"),
        name="rel_bias_tiles",
    )(rel_bias, jnp.asarray(near_ids), jnp.asarray(meta_ids))


def _attn_kernel(qT_ref, k_ref, vT_ref, near_ref, meta_ref, lam_ref, sw_ref, o_ref,
                 m_ref, l_ref, acc_ref, *, n_real, lambda_init):
    qi = pl.program_id(2)
    qT = qT_ref[0]
    sub = lax.broadcasted_iota(jnp.int32, (DIFF_VDIM, 1), 0)
    q_maps = (jnp.where(sub < DIFF_HEAD_DIM, qT, jnp.zeros_like(qT)),
              jnp.where(sub >= DIFF_HEAD_DIM, qT, jnp.zeros_like(qT)))

    def scores(k_blk):
        return [jnp.dot(k_blk, q, preferred_element_type=F32) for q in q_maps]

    def update(s, vT_blk):
        for c in range(2):
            m_old = m_ref[c]
            m_new = jnp.maximum(m_old, jnp.max(s[c], axis=0, keepdims=True))
            alpha = jnp.exp(m_old - m_new)
            p = jnp.exp(s[c] - m_new)
            l_ref[c] = alpha * l_ref[c] + jnp.sum(p, axis=0, keepdims=True)
            acc_ref[c] = alpha * acc_ref[c] + jnp.dot(vT_blk, p.astype(BF16), preferred_element_type=F32)
            m_ref[c] = m_new

    meta_lo = n_real + CHUNK - META_KEYS
    k_m = k_ref[meta_lo:meta_lo + META_KEYS, :]
    vT_m = vT_ref[0, :, meta_lo:meta_lo + META_KEYS]
    bias_m = meta_ref[0, jnp.minimum(qi, 1)]
    s = [sc + bias_m for sc in scores(k_m)]
    for c in range(2):
        m_c = jnp.max(s[c], axis=0, keepdims=True)
        p = jnp.exp(s[c] - m_c)
        m_ref[c] = m_c
        l_ref[c] = jnp.sum(p, axis=0, keepdims=True)
        acc_ref[c] = jnp.dot(vT_m, p.astype(BF16), preferred_element_type=F32)

    def far_step(j, carry):
        lo = pl.multiple_of(j * TK, TK)
        update(scores(k_ref[pl.ds(lo, TK), :]), vT_ref[0, :, pl.ds(lo, TK)])
        return carry

    n_far = jnp.maximum(2 * qi - 1, 0)
    lax.fori_loop(0, n_far, far_step, 0)

    def near_step(t):
        lo = pl.multiple_of((2 * qi - 1 + t) * TK, TK)
        s = scores(k_ref[pl.ds(lo, TK), :])
        update([sc + near_ref[0, t] for sc in s], vT_ref[0, :, pl.ds(lo, TK)])

    pl.when(qi > 0)(lambda: near_step(0))
    near_step(1)
    near_step(2)

    lv = lam_ref[...]
    lam = (jnp.exp(jnp.sum(lv[0:1] * lv[1:2], axis=-1, keepdims=True))
           - jnp.exp(jnp.sum(lv[2:3] * lv[3:4], axis=-1, keepdims=True)) + lambda_init)
    o = acc_ref[0] / l_ref[0] - lam * (acc_ref[1] / l_ref[1])
    o = o * lax.rsqrt(jnp.mean(o * o, axis=0, keepdims=True) + EPS)
    o = o * sw_ref[...] * (1.0 - lambda_init)
    o_ref[...] = o.T.astype(BF16)


def _attention(qT, k, vT, near, meta, lamv, sw, *, batch, n_real, lp, lambda_init):
    nq = n_real // TQ
    return pl.pallas_call(
        functools.partial(_attn_kernel, n_real=n_real, lambda_init=lambda_init),
        grid=(batch, DIFF_HEADS, nq),
        in_specs=[
            pl.BlockSpec((1, DIFF_VDIM, TQ), lambda b, h, i: (b, h, i)),
            pl.BlockSpec((lp, DIFF_VDIM), lambda b, h, i: (b, h)),
            pl.BlockSpec((1, DIFF_VDIM, lp), lambda b, h, i: (b, h, 0)),
            pl.BlockSpec((1, 3, TK, TQ), lambda b, h, i: (h, 0, 0, 0)),
            pl.BlockSpec((1, 2, META_KEYS, TQ), lambda b, h, i: (h, 0, 0, 0)),
            _resident((4, DIFF_HEAD_DIM)),
            _resident((DIFF_VDIM, 1)),
        ],
        out_specs=pl.BlockSpec((TQ, DIFF_VDIM), lambda b, h, i: (b * nq + i, h)),
        out_shape=jax.ShapeDtypeStruct((batch * n_real, D_MODEL), BF16),
        scratch_shapes=[
            pltpu.VMEM((2, 1, TQ), F32),
            pltpu.VMEM((2, 1, TQ), F32),
            pltpu.VMEM((2, DIFF_VDIM, TQ), F32),
        ],
        compiler_params=_params("parallel", "parallel", "arbitrary"),
        name="diff_attention",
    )(qT, k, vT, near, meta, lamv, sw)


def _mm_res3_kernel(a_ref, w_ref, r_ref, o_ref):
    o_ref[...] = r_ref[0] + jnp.dot(a_ref[...], w_ref[...], preferred_element_type=F32)


def _mm_res_real(a, w, res3, *, batch, n_real):
    tiles = n_real // ROW_TILE
    k = a.shape[1]
    return pl.pallas_call(
        _mm_res3_kernel,
        grid=(batch, tiles),
        in_specs=[
            pl.BlockSpec((ROW_TILE, k), lambda b, i: (b * tiles + i, 0)),
            _resident((k, D_MODEL)),
            pl.BlockSpec((1, ROW_TILE, D_MODEL), lambda b, i: (b, i, 0)),
        ],
        out_specs=pl.BlockSpec((ROW_TILE, D_MODEL), lambda b, i: (b * tiles + i, 0)),
        out_shape=jax.ShapeDtypeStruct((batch * n_real, D_MODEL), F32),
        compiler_params=_params("parallel", "parallel"),
        name="attn_out_residual",
    )(a, w, res3)


def kernel(x, meta_tokens, norm_w, ssm_in_w, ssm_conv_w, ssm_conv_b, ssm_dt_bias, ssm_a_log, ssm_d,
           ssm_norm_w, ssm_out_w, kv_norm_w, w_kv, w_q, lam_q1, lam_k1, lam_q2, lam_k2, subln_w,
           w_o, rel_bias, ffn_w_gu, ffn_w_down, final_norm_w):
    batch, n_real, d = x.shape
    assert d == D_MODEL and n_real % TQ == 0 and n_real % ROW_TILE == 0 and TQ == 2 * TK
    assert norm_w.shape[0] == 2 and ssm_in_w.shape[0] == 1 and w_q.shape[0] == 1
    lp = n_real + CHUNK
    n_pad = CHUNK - N_META
    n_chunks = lp // CHUNK
    rows = batch * lp
    assert rows % ROW_TILE == 0
    row2 = lambda v: v.reshape(1, -1).astype(F32)

    tail = jnp.concatenate([jnp.zeros((n_pad, D_MODEL), F32), meta_tokens.astype(F32)], axis=0)
    h0 = jnp.concatenate([x, jnp.broadcast_to(tail[None], (batch, CHUNK, D_MODEL))], axis=1)
    h0 = h0.reshape(rows, D_MODEL)

    in_w = ssm_in_w[0]
    w_zx = in_w[:, :D_INNER + CONV_DIM].astype(BF16)
    w_dt = in_w[:, D_INNER + CONV_DIM:]
    z, xbc, dt, dtT = _in_proj(h0, row2(norm_w[0, 0]), w_zx, w_dt, w_dt.T)
    y = _ssd(z, xbc, dt, dtT, ssm_conv_w[0], row2(ssm_conv_b[0]),
             row2(ssm_dt_bias[0]), ssm_dt_bias[0].reshape(-1, 1),
             row2(ssm_a_log[0]), ssm_a_log[0].reshape(-1, 1),
             row2(jnp.repeat(ssm_d[0], SSM_HEADDIM)), row2(ssm_norm_w[0]),
             batch=batch, n_chunks=n_chunks, n_pad=n_pad)
    h1 = _mm_res(y, ssm_out_w[0].astype(BF16), h0, rows=rows)
    gu0 = ffn_w_gu[0].astype(BF16)
    h2 = _ffn(h1, row2(norm_w[0, 1]), gu0[:, :FFN_HIDDEN], gu0[:, FFN_HIDDEN:],
              ffn_w_down[0].astype(BF16), row2(final_norm_w), final_norm=False)

    layer = 1
    lambda_init = 0.8 - 0.6 * math.exp(-0.3 * layer)
    k, vT, qT = _kvq(h2, row2(kv_norm_w), row2(norm_w[1, 0]), w_kv[:, :D_MODEL].astype(BF16),
                     w_kv[:, D_MODEL:].T.astype(BF16), w_q[0].T.astype(BF16), batch=batch, lp=lp)
    near, meta = _bias_tiles(rel_bias.astype(F32))
    lamv = jnp.stack([lam_q1[0], lam_k1[0], lam_q2[0], lam_k2[0]]).astype(F32)
    attn = _attention(qT, k, vT, near, meta, lamv, subln_w[0].reshape(-1, 1).astype(F32),
                      batch=batch, n_real=n_real, lp=lp, lambda_init=lambda_init)
    h3 = _mm_res_real(attn, w_o[0].astype(BF16), h2.reshape(batch, lp, D_MODEL),
                      batch=batch, n_real=n_real)
    gu1 = ffn_w_gu[1].astype(BF16)
    out = _ffn(h3, row2(norm_w[1, 1]), gu1[:, :FFN_HIDDEN], gu1[:, FFN_HIDDEN:],
               ffn_w_down[1].astype(BF16), row2(final_norm_w), final_norm=True)
    return out.reshape(batch, n_real, D_MODEL)
```

```python
import functools
import math

import numpy as np
import jax
import jax.numpy as jnp
from jax import lax
from jax.experimental import pallas as pl
from jax.experimental.pallas import tpu as pltpu

F32 = jnp.float32
BF16 = jnp.bfloat16
HIGHEST = lax.Precision.HIGHEST

D_MODEL = 1024
N_META = 16
EPS = 1e-6
NEG_INF = -1e30
SSM_HEADDIM = 64
SSM_HEADS = 32
SSM_GROUPS = 8
SSM_HPG = SSM_HEADS // SSM_GROUPS
SSM_STATE = 128
SSM_CONV = 4
D_INNER = SSM_HEADS * SSM_HEADDIM
GROUP_W = D_INNER // SSM_GROUPS
CONV_DIM = D_INNER + 2 * SSM_GROUPS * SSM_STATE
DIFF_HEADS = 8
DIFF_HEAD_DIM = 64
DIFF_VDIM = 2 * DIFF_HEAD_DIM
REL_BUCKETS = 32
REL_MAX_DIST = 128
FFN_HIDDEN = 2816

SUBLANES = 8
LANES = 128
CHUNK = 256
ROW_TILE = 512
MXU_N = 256
TQ = 512
TK = 512
BAND = REL_MAX_DIST
META_KEYS = 128
V_ROWS = DIFF_VDIM + 16
LOG2E = math.log2(math.e)
FFN_TILE = 1408
VMEM_LIMIT = 56 * 1024 * 1024


def _resident(shape):
    nd = len(shape)
    return pl.BlockSpec(shape, lambda *_: (0,) * nd, pipeline_mode=pl.Buffered(1))


def _params(*sem):
    return pltpu.CompilerParams(dimension_semantics=sem, vmem_limit_bytes=VMEM_LIMIT)


def _rmsnorm(x, w):
    return x * lax.rsqrt(jnp.mean(x * x, axis=-1, keepdims=True) + EPS) * w


def _softplus(v):
    return jnp.maximum(v, 0.0) + jnp.log(1.0 + jnp.exp(-jnp.abs(v)))


def _silu(v):
    return v * jax.nn.sigmoid(v)


_NT = (((1,), (1,)), ((), ()))


def _in_proj_kernel(x_ref, nw_ref, w_ref, wdt_ref, wdtT_ref, z_ref, xbc_ref, dt_ref, dtT_ref):
    xn = _rmsnorm(x_ref[...], nw_ref[...])
    xb = xn.astype(BF16)
    for c0 in range(0, D_INNER, 1024):
        z_ref[:, c0:c0 + 1024] = jnp.dot(
            xb, w_ref[:, c0:c0 + 1024], preferred_element_type=F32).astype(BF16)
    for c0 in range(0, CONV_DIM, 1024):
        xbc_ref[:, c0:c0 + 1024] = jnp.dot(
            xb, w_ref[:, D_INNER + c0:D_INNER + c0 + 1024], preferred_element_type=F32).astype(BF16)
    dt_ref[...] = jnp.dot(xn, wdt_ref[...], precision=HIGHEST, preferred_element_type=F32)
    dtT_ref[...] = lax.dot_general(wdtT_ref[...], xn, _NT, precision=HIGHEST,
                                   preferred_element_type=F32)


def _in_proj(h2d, nw, w_zx, w_dt, w_dtT):
    rows = h2d.shape[0]
    grid = (rows // ROW_TILE,)
    return pl.pallas_call(
        _in_proj_kernel,
        grid=grid,
        in_specs=[
            pl.BlockSpec((ROW_TILE, D_MODEL), lambda i: (i, 0)),
            _resident((1, D_MODEL)),
            _resident((D_MODEL, D_INNER + CONV_DIM)),
            _resident((D_MODEL, SSM_HEADS)),
            _resident((SSM_HEADS, D_MODEL)),
        ],
        out_specs=[
            pl.BlockSpec((ROW_TILE, D_INNER), lambda i: (i, 0)),
            pl.BlockSpec((ROW_TILE, CONV_DIM), lambda i: (i, 0)),
            pl.BlockSpec((ROW_TILE, SSM_HEADS), lambda i: (i, 0)),
            pl.BlockSpec((SSM_HEADS, ROW_TILE), lambda i: (0, i)),
        ],
        out_shape=[
            jax.ShapeDtypeStruct((rows, D_INNER), BF16),
            jax.ShapeDtypeStruct((rows, CONV_DIM), BF16),
            jax.ShapeDtypeStruct((rows, SSM_HEADS), F32),
            jax.ShapeDtypeStruct((SSM_HEADS, rows), F32),
        ],
        compiler_params=_params("parallel"),
        name="in_proj",
    )(h2d, nw, w_zx, w_dt, w_dtT)


def _ssd_kernel(z_ref, xbc_ref, dt_ref, dtT_ref, cw_ref, cb_ref, dtb_ref, dtbT_ref, alog_ref,
                alogT_ref, dsk_ref, nw_ref, y_ref, state_ref, xe_ref, xc_ref, *, n_pad):
    Q = CHUNK
    c = pl.program_id(1)

    @pl.when(c == 0)
    def _():
        state_ref[...] = jnp.zeros_like(state_ref)
        xe_ref[0:SUBLANES, :] = jnp.zeros((SUBLANES, CONV_DIM), F32)

    xe_ref[SUBLANES:SUBLANES + Q, :] = xbc_ref[...].astype(F32)

    def conv_cols(i, carry):
        cs = pl.ds(pl.multiple_of(i * 512, 512), 512)
        acc = cb_ref[:, cs] + cw_ref[3:4, cs] * xe_ref[SUBLANES:SUBLANES + Q, cs]
        for k in range(SSM_CONV - 1):
            off = SUBLANES - (SSM_CONV - 1) + k
            acc = acc + cw_ref[k:k + 1, cs] * xe_ref[off:off + Q, cs]
        xc_ref[:, cs] = _silu(acc)
        return carry

    lax.fori_loop(0, CONV_DIM // 512, conv_cols, 0)
    xe_ref[0:SUBLANES, :] = xe_ref[Q:Q + SUBLANES, :]

    first_valid = jnp.where(c == 0, n_pad, 0)
    row = lax.broadcasted_iota(jnp.int32, (Q, 1), 0)
    col = lax.broadcasted_iota(jnp.int32, (1, Q), 1)
    dtc = jnp.where(row >= first_valid, _softplus(dt_ref[...] + dtb_ref[...]), 0.0)
    dtr = jnp.where(col >= first_valid, _softplus(dtT_ref[...] + dtbT_ref[...]), 0.0)
    dac = dtc * (-jnp.exp(alog_ref[...]))
    dar = dtr * (-jnp.exp(alogT_ref[...]))
    ii = lax.broadcasted_iota(jnp.int32, (Q, Q), 0)
    jj = lax.broadcasted_iota(jnp.int32, (Q, Q), 1)
    tri = ii >= jj
    acs_c = jnp.dot(tri.astype(F32), dac, precision=HIGHEST, preferred_element_type=F32)
    acs_r = jnp.dot(dar, (ii <= jj).astype(F32), precision=HIGHEST, preferred_element_type=F32)
    eacs_c = jnp.exp(acs_c)
    lane = lax.broadcasted_iota(jnp.int32, (1, GROUP_W), 1)

    for g in range(SSM_GROUPS):
        b_g = xc_ref[:, D_INNER + g * SSM_STATE:D_INNER + (g + 1) * SSM_STATE]
        c_g = xc_ref[:, D_INNER + (SSM_GROUPS + g) * SSM_STATE:D_INNER + (SSM_GROUPS + g + 1) * SSM_STATE]
        x_g = xc_ref[:, g * GROUP_W:(g + 1) * GROUP_W]
        c_b = c_g.astype(BF16)
        cb = lax.dot_general(c_b, b_g.astype(BF16), _NT, preferred_element_type=F32)
        b_t = b_g.T
        s_old = state_ref[g]
        y_off = jnp.dot(c_b, s_old.astype(BF16), preferred_element_type=F32)
        y_diag = jnp.zeros((Q, GROUP_W), F32)
        s_add = jnp.zeros((SSM_STATE, GROUP_W), F32)
        scale = jnp.zeros((Q, GROUP_W), F32)
        sdec = jnp.zeros((1, GROUP_W), F32)
        for r in range(SSM_HPG):
            h = g * SSM_HPG + r
            ac = acs_c[:, h:h + 1]
            ar = acs_r[h:h + 1, :]
            dt_h = dtr[h:h + 1, :]
            decay = jnp.exp(jnp.where(tri, ac - ar, NEG_INF))
            w = (cb * decay * dt_h).astype(BF16)
            in_head = (lane >= r * SSM_HEADDIM) & (lane < (r + 1) * SSM_HEADDIM)
            x_r = jnp.where(in_head, x_g, 0.0).astype(BF16)
            y_diag = y_diag + jnp.dot(w, x_r, preferred_element_type=F32)
            a_last = ar[:, Q - 1:Q]
            w_state = jnp.exp(a_last - ar) * dt_h
            s_add = s_add + jnp.dot((b_t * w_state).astype(BF16), x_r, preferred_element_type=F32)
            scale = jnp.where(in_head, eacs_c[:, h:h + 1], scale)
            sdec = jnp.where(in_head, jnp.exp(a_last), sdec)
        state_ref[g] = s_old * sdec + s_add
        gs = slice(g * GROUP_W, (g + 1) * GROUP_W)
        y = y_diag + y_off * scale + x_g * dsk_ref[:, gs]
        y = y * _silu(z_ref[:, gs].astype(F32))
        y_ref[:, gs] = _rmsnorm(y, nw_ref[:, gs]).astype(BF16)


def _ssd(z, xbc, dt, dtT, cw, cb, dtb, dtbT, alog, alogT, dsk, nw, *, batch, n_chunks, n_pad):
    rows = z.shape[0]
    blk = lambda b, c: b * n_chunks + (c + n_chunks - 1) % n_chunks
    return pl.pallas_call(
        functools.partial(_ssd_kernel, n_pad=n_pad),
        grid=(batch, n_chunks),
        in_specs=[
            pl.BlockSpec((CHUNK, D_INNER), lambda b, c: (blk(b, c), 0)),
            pl.BlockSpec((CHUNK, CONV_DIM), lambda b, c: (blk(b, c), 0)),
            pl.BlockSpec((CHUNK, SSM_HEADS), lambda b, c: (blk(b, c), 0)),
            pl.BlockSpec((SSM_HEADS, CHUNK), lambda b, c: (0, blk(b, c))),
            _resident((SSM_CONV, CONV_DIM)),
            _resident((1, CONV_DIM)),
            _resident((1, SSM_HEADS)),
            _resident((SSM_HEADS, 1)),
            _resident((1, SSM_HEADS)),
            _resident((SSM_HEADS, 1)),
            _resident((1, D_INNER)),
            _resident((1, D_INNER)),
        ],
        out_specs=pl.BlockSpec((CHUNK, D_INNER), lambda b, c: (blk(b, c), 0)),
        out_shape=jax.ShapeDtypeStruct((rows, D_INNER), BF16),
        scratch_shapes=[
            pltpu.VMEM((SSM_GROUPS, SSM_STATE, GROUP_W), F32),
            pltpu.VMEM((CHUNK + 2 * SUBLANES, CONV_DIM), F32),
            pltpu.VMEM((CHUNK, CONV_DIM), F32),
        ],
        compiler_params=_params("arbitrary", "arbitrary"),
        name="ssd_scan",
    )(z, xbc, dt, dtT, cw, cb, dtb, dtbT, alog, alogT, dsk, nw)


def _mm_res_kernel(a_ref, w_ref, r_ref, o_ref):
    o_ref[...] = r_ref[...] + jnp.dot(a_ref[...], w_ref[...], preferred_element_type=F32)


def _mm_res(a, w, res, *, rows):
    k = a.shape[1]
    return pl.pallas_call(
        _mm_res_kernel,
        grid=(rows // ROW_TILE,),
        in_specs=[
            pl.BlockSpec((ROW_TILE, k), lambda i: (i, 0)),
            _resident((k, D_MODEL)),
            pl.BlockSpec((ROW_TILE, D_MODEL), lambda i: (i, 0)),
        ],
        out_specs=pl.BlockSpec((ROW_TILE, D_MODEL), lambda i: (i, 0)),
        out_shape=jax.ShapeDtypeStruct((rows, D_MODEL), F32),
        compiler_params=_params("parallel"),
        name="proj_residual",
    )(a, w, res)


def _ffn_kernel(h_ref, nw_ref, wg_ref, wv_ref, wd_ref, fnw_ref, o_ref, *, final_norm):
    x = h_ref[...]
    xb = _rmsnorm(x, nw_ref[...]).astype(BF16)
    acc = x
    for t0 in range(0, FFN_HIDDEN, FFN_TILE):
        ts = slice(t0, t0 + FFN_TILE)
        gate = jnp.dot(xb, wg_ref[:, ts], preferred_element_type=F32)
        val = jnp.dot(xb, wv_ref[:, ts], preferred_element_type=F32)
        act = (_silu(gate) * val).astype(BF16)
        acc = acc + jnp.dot(act, wd_ref[ts, :], preferred_element_type=F32)
    if final_norm:
        acc = _rmsnorm(acc, fnw_ref[...])
    o_ref[...] = acc


def _ffn(h, nw, wg, wv, wd, fnw, *, final_norm):
    rows = h.shape[0]
    return pl.pallas_call(
        functools.partial(_ffn_kernel, final_norm=final_norm),
        grid=(rows // ROW_TILE,),
        in_specs=[
            pl.BlockSpec((ROW_TILE, D_MODEL), lambda i: (i, 0)),
            _resident((1, D_MODEL)),
            _resident((D_MODEL, FFN_HIDDEN)),
            _resident((D_MODEL, FFN_HIDDEN)),
            _resident((FFN_HIDDEN, D_MODEL)),
            _resident((1, D_MODEL)),
        ],
        out_specs=pl.BlockSpec((ROW_TILE, D_MODEL), lambda i: (i, 0)),
        out_shape=jax.ShapeDtypeStruct((rows, D_MODEL), F32),
        compiler_params=_params("parallel"),
        name="ffn",
    )(h, nw, wg, wv, wd, fnw)


def _kvq_kernel(h_ref, kvnw_ref, qnw_ref, wk_ref, wvT_ref, wqT_ref, k_ref, vT_ref, qT_ref):
    x = h_ref[...]
    inv = lax.rsqrt(jnp.mean(x * x, axis=-1, keepdims=True) + EPS)
    xkv = (x * inv * kvnw_ref[...]).astype(BF16)
    xq = (x * inv * qnw_ref[...]).astype(BF16)
    k_ref[...] = jnp.dot(xkv, wk_ref[...], preferred_element_type=F32).astype(BF16)
    vT_ref[0] = lax.dot_general(wvT_ref[...], xkv, _NT, preferred_element_type=F32).astype(BF16)
    qT = lax.dot_general(wqT_ref[...], xq, _NT, preferred_element_type=F32)
    qT_ref[0] = (qT * (DIFF_HEAD_DIM ** -0.5 * LOG2E)).astype(BF16)


def _kvq(h, kvnw, qnw, wk, wvT, wqT, *, batch, lp):
    tiles = lp // CHUNK
    return pl.pallas_call(
        _kvq_kernel,
        grid=(batch, tiles),
        in_specs=[
            pl.BlockSpec((CHUNK, D_MODEL), lambda b, i: (b * tiles + i, 0)),
            _resident((1, D_MODEL)),
            _resident((1, D_MODEL)),
            _resident((D_MODEL, D_MODEL)),
            _resident((D_MODEL, D_MODEL)),
            _resident((D_MODEL, D_MODEL)),
        ],
        out_specs=[
            pl.BlockSpec((CHUNK, D_MODEL), lambda b, i: (b * tiles + i, 0)),
            pl.BlockSpec((1, D_MODEL, CHUNK), lambda b, i: (b, 0, i)),
            pl.BlockSpec((1, D_MODEL, CHUNK), lambda b, i: (b, 0, i)),
        ],
        out_shape=[
            jax.ShapeDtypeStruct((batch * lp, D_MODEL), BF16),
            jax.ShapeDtypeStruct((batch, D_MODEL, lp), BF16),
            jax.ShapeDtypeStruct((batch, D_MODEL, lp), BF16),
        ],
        compiler_params=_params("parallel", "parallel"),
        name="kvq_proj",
    )(h, kvnw, qnw, wk, wvT, wqT)


def _rel_bucket_np(n):
    n = np.asarray(n)
    max_exact = REL_BUCKETS // 2
    nf = np.maximum(n, 1).astype(np.float32)
    large = max_exact + (np.log(nf / np.float32(max_exact)) / np.float32(math.log(REL_MAX_DIST / max_exact))
                         * np.float32(REL_BUCKETS - max_exact)).astype(np.int32)
    large = np.minimum(large, REL_BUCKETS - 1)
    return np.where(n < max_exact, n, large).astype(np.int32)


def _bucket_tiles():
    i = np.arange(TK)[:, None]
    j = np.arange(TQ)[None, :]
    diag = np.where(j - i >= 0, _rel_bucket_np(np.maximum(j - i, 0)), -1)
    a = np.arange(BAND)[:, None]
    b = np.arange(BAND)[None, :]
    corner = _rel_bucket_np(b - a + BAND)
    m = np.arange(META_KEYS)[:, None] - (META_KEYS - N_META)
    dist0 = N_META + j - m
    meta0 = np.where(m >= 0, _rel_bucket_np(np.maximum(dist0, 0)), -1)
    meta_far = np.where(m >= 0, REL_BUCKETS - 1, -1) + 0 * j
    return (corner.astype(np.int32), diag.astype(np.int32),
            np.stack([meta0, meta_far]).astype(np.int32))


def _bias_kernel(tab_ref, corner_id_ref, diag_id_ref, meta_id_ref, corner_ref, diag_ref, meta_ref):
    h = pl.program_id(0)
    far = tab_ref[REL_BUCKETS - 1, h]

    def build(ids):
        out = jnp.where(ids < 0, NEG_INF, 0.0).astype(F32)
        for b in range(REL_BUCKETS - 1):
            out = jnp.where(ids == b, (tab_ref[b, h] - far) * LOG2E, out)
        return out

    corner_ref[0] = build(corner_id_ref[...])
    diag_ref[0] = build(diag_id_ref[...])
    meta_ref[0] = build(meta_id_ref[...])


def _bias_tiles(rel_bias):
    corner_ids, diag_ids, meta_ids = _bucket_tiles()
    return pl.pallas_call(
        _bias_kernel,
        grid=(DIFF_HEADS,),
        in_specs=[
            pl.BlockSpec(memory_space=pltpu.SMEM),
            _resident((BAND, BAND)),
            _resident((TK, TQ)),
            _resident((2, META_KEYS, TQ)),
        ],
        out_specs=[
            pl.BlockSpec((1, BAND, BAND), lambda h: (h, 0, 0)),
            pl.BlockSpec((1, TK, TQ), lambda h: (h, 0, 0)),
            pl.BlockSpec((1, 2, META_KEYS, TQ), lambda h: (h, 0, 0, 0)),
        ],
        out_shape=[
            jax.ShapeDtypeStruct((DIFF_HEADS, BAND, BAND), F32),
            jax.ShapeDtypeStruct((DIFF_HEADS, TK, TQ), F32),
            jax.ShapeDtypeStruct((DIFF_HEADS, 2, META_KEYS, TQ), F32),
        ],
        compiler_params=_params("arbitrary"),
        name="rel_bias_tiles",
    )(rel_bias, jnp.asarray(corner_ids), jnp.asarray(diag_ids), jnp.asarray(meta_ids))


def _attn_kernel(qT_ref, k_ref, vT_ref, corner_ref, diag_ref, meta_ref, lam_ref, sw_ref, o_ref,
                 s_ref, m_ref, acc_ref, *, n_real, lambda_init):
    qi = pl.program_id(2)
    qT = qT_ref[0]
    sub = lax.broadcasted_iota(jnp.int32, (DIFF_VDIM, 1), 0)
    q_maps = (jnp.where(sub < DIFF_HEAD_DIM, qT, jnp.zeros_like(qT)),
              jnp.where(sub >= DIFF_HEAD_DIM, qT, jnp.zeros_like(qT)))
    chains = [(c, slice(qh * MXU_N, (qh + 1) * MXU_N)) for c in range(2) for qh in range(TQ // MXU_N)]

    def v_ext(lo, n):
        return jnp.concatenate([vT_ref[0, :, pl.ds(lo, n)],
                                jnp.ones((V_ROWS - DIFF_VDIM, n), BF16)], axis=0)

    def scores_into_s(lo):
        k_blk = k_ref[pl.ds(lo, TK), :]
        for c, qs in chains:
            s_ref[c, :, qs] = jnp.dot(k_blk, q_maps[c][:, qs], preferred_element_type=F32)

    def step(lo, next_lo, bias_ref):
        vx = v_ext(lo, TK)
        k_next = None if next_lo is None else k_ref[pl.ds(next_lo, TK), :]
        for c, qs in chains:
            s = s_ref[c, :, qs]
            if bias_ref is not None:
                s = s + bias_ref[0, :, qs]
            m_old = m_ref[c, :, qs]
            m_new = jnp.maximum(m_old, jnp.max(s, axis=0, keepdims=True))
            alpha = jnp.exp2(m_old - m_new)
            p = jnp.exp2(s - m_new).astype(BF16)
            acc_ref[c, :, qs] = alpha * acc_ref[c, :, qs] + jnp.dot(vx, p, preferred_element_type=F32)
            m_ref[c, :, qs] = m_new
            if k_next is not None:
                s_ref[c, :, qs] = jnp.dot(k_next, q_maps[c][:, qs], preferred_element_type=F32)

    meta_lo = n_real + CHUNK - META_KEYS
    k_m = k_ref[meta_lo:meta_lo + META_KEYS, :]
    vx_m = v_ext(meta_lo, META_KEYS)
    bias_m = meta_ref[0, jnp.minimum(qi, 1)]
    for c in range(2):
        s = jnp.dot(k_m, q_maps[c], preferred_element_type=F32) + bias_m
        m_c = jnp.max(s, axis=0, keepdims=True)
        m_ref[c] = m_c
        acc_ref[c] = jnp.dot(vx_m, jnp.exp2(s - m_c).astype(BF16), preferred_element_type=F32)

    scores_into_s(0)

    def far_step(js, carry):
        lo = pl.multiple_of(js * TK, TK)
        step(lo, lo + TK, None)
        return carry

    lax.fori_loop(0, jnp.maximum(qi - 1, 0), far_step, 0)

    @pl.when(qi > 0)
    def _():
        for c in range(2):
            s_ref[c, TK - BAND:TK, 0:BAND] = s_ref[c, TK - BAND:TK, 0:BAND] + corner_ref[0]
        lo = pl.multiple_of((qi - 1) * TK, TK)
        step(lo, lo + TK, None)

    step(pl.multiple_of(qi * TK, TK), None, diag_ref)

    lv = lam_ref[...]
    lam = (jnp.exp(jnp.sum(lv[0:1] * lv[1:2], axis=-1, keepdims=True))
           - jnp.exp(jnp.sum(lv[2:3] * lv[3:4], axis=-1, keepdims=True)) + lambda_init)
    o = [acc_ref[c, 0:DIFF_VDIM, :] / acc_ref[c, DIFF_VDIM:DIFF_VDIM + 1, :] for c in range(2)]
    o = o[0] - lam * o[1]
    o = o * lax.rsqrt(jnp.mean(o * o, axis=0, keepdims=True) + EPS)
    o = o * sw_ref[...] * (1.0 - lambda_init)
    o_ref[...] = o.T.astype(BF16)


def _attention(qT, k, vT, corner, diag, meta, lamv, sw, *, batch, n_real, lp, lambda_init):
    nq = n_real // TQ
    return pl.pallas_call(
        functools.partial(_attn_kernel, n_real=n_real, lambda_init=lambda_init),
        grid=(batch, DIFF_HEADS, nq),
        in_specs=[
            pl.BlockSpec((1, DIFF_VDIM, TQ), lambda b, h, i: (b, h, i)),
            pl.BlockSpec((lp, DIFF_VDIM), lambda b, h, i: (b, h)),
            pl.BlockSpec((1, DIFF_VDIM, lp), lambda b, h, i: (b, h, 0)),
            pl.BlockSpec((1, BAND, BAND), lambda b, h, i: (h, 0, 0)),
            pl.BlockSpec((1, TK, TQ), lambda b, h, i: (h, 0, 0)),
            pl.BlockSpec((1, 2, META_KEYS, TQ), lambda b, h, i: (h, 0, 0, 0)),
            _resident((4, DIFF_HEAD_DIM)),
            _resident((DIFF_VDIM, 1)),
        ],
        out_specs=pl.BlockSpec((TQ, DIFF_VDIM), lambda b, h, i: (b * nq + i, h)),
        out_shape=jax.ShapeDtypeStruct((batch * n_real, D_MODEL), BF16),
        scratch_shapes=[
            pltpu.VMEM((2, TK, TQ), F32),
            pltpu.VMEM((2, 1, TQ), F32),
            pltpu.VMEM((2, V_ROWS, TQ), F32),
        ],
        compiler_params=_params("parallel", "parallel", "arbitrary"),
        name="diff_attention",
    )(qT, k, vT, corner, diag, meta, lamv, sw)


def _mm_res3_kernel(a_ref, w_ref, r_ref, o_ref):
    o_ref[...] = r_ref[0] + jnp.dot(a_ref[...], w_ref[...], preferred_element_type=F32)


def _mm_res_real(a, w, res3, *, batch, n_real):
    tiles = n_real // ROW_TILE
    k = a.shape[1]
    return pl.pallas_call(
        _mm_res3_kernel,
        grid=(batch, tiles),
        in_specs=[
            pl.BlockSpec((ROW_TILE, k), lambda b, i: (b * tiles + i, 0)),
            _resident((k, D_MODEL)),
            pl.BlockSpec((1, ROW_TILE, D_MODEL), lambda b, i: (b, i, 0)),
        ],
        out_specs=pl.BlockSpec((ROW_TILE, D_MODEL), lambda b, i: (b * tiles + i, 0)),
        out_shape=jax.ShapeDtypeStruct((batch * n_real, D_MODEL), F32),
        compiler_params=_params("parallel", "parallel"),
        name="attn_out_residual",
    )(a, w, res3)


def kernel(x, meta_tokens, norm_w, ssm_in_w, ssm_conv_w, ssm_conv_b, ssm_dt_bias, ssm_a_log, ssm_d,
           ssm_norm_w, ssm_out_w, kv_norm_w, w_kv, w_q, lam_q1, lam_k1, lam_q2, lam_k2, subln_w,
           w_o, rel_bias, ffn_w_gu, ffn_w_down, final_norm_w):
    batch, n_real, d = x.shape
    assert d == D_MODEL and n_real % TQ == 0 and n_real % ROW_TILE == 0 and TQ == TK
    assert norm_w.shape[0] == 2 and ssm_in_w.shape[0] == 1 and w_q.shape[0] == 1
    lp = n_real + CHUNK
    n_pad = CHUNK - N_META
    n_chunks = lp // CHUNK
    rows = batch * lp
    assert rows % ROW_TILE == 0
    row2 = lambda v: v.reshape(1, -1).astype(F32)

    tail = jnp.concatenate([jnp.zeros((n_pad, D_MODEL), F32), meta_tokens.astype(F32)], axis=0)
    h0 = jnp.concatenate([x, jnp.broadcast_to(tail[None], (batch, CHUNK, D_MODEL))], axis=1)
    h0 = h0.reshape(rows, D_MODEL)

    in_w = ssm_in_w[0]
    w_zx = in_w[:, :D_INNER + CONV_DIM].astype(BF16)
    w_dt = in_w[:, D_INNER + CONV_DIM:]
    z, xbc, dt, dtT = _in_proj(h0, row2(norm_w[0, 0]), w_zx, w_dt, w_dt.T)
    y = _ssd(z, xbc, dt, dtT, ssm_conv_w[0], row2(ssm_conv_b[0]),
             row2(ssm_dt_bias[0]), ssm_dt_bias[0].reshape(-1, 1),
             row2(ssm_a_log[0]), ssm_a_log[0].reshape(-1, 1),
             row2(jnp.repeat(ssm_d[0], SSM_HEADDIM)), row2(ssm_norm_w[0]),
             batch=batch, n_chunks=n_chunks, n_pad=n_pad)
    h1 = _mm_res(y, ssm_out_w[0].astype(BF16), h0, rows=rows)
    gu0 = ffn_w_gu[0].astype(BF16)
    h2 = _ffn(h1, row2(norm_w[0, 1]), gu0[:, :FFN_HIDDEN], gu0[:, FFN_HIDDEN:],
              ffn_w_down[0].astype(BF16), row2(final_norm_w), final_norm=False)

    layer = 1
    lambda_init = 0.8 - 0.6 * math.exp(-0.3 * layer)
    k, vT, qT = _kvq(h2, row2(kv_norm_w), row2(norm_w[1, 0]), w_kv[:, :D_MODEL].astype(BF16),
                     w_kv[:, D_MODEL:].T.astype(BF16), w_q[0].T.astype(BF16), batch=batch, lp=lp)
    corner, diag, meta = _bias_tiles(rel_bias.astype(F32))
    lamv = jnp.stack([lam_q1[0], lam_k1[0], lam_q2[0], lam_k2[0]]).astype(F32)
    attn = _attention(qT, k, vT, corner, diag, meta, lamv, subln_w[0].reshape(-1, 1).astype(F32),
                      batch=batch, n_real=n_real, lp=lp, lambda_init=lambda_init)
    h3 = _mm_res_real(attn, w_o[0].astype(BF16), h2.reshape(batch, lp, D_MODEL),
                      batch=batch, n_real=n_real)
    gu1 = ffn_w_gu[1].astype(BF16)
    out = _ffn(h3, row2(norm_w[1, 1]), gu1[:, :FFN_HIDDEN], gu1[:, FFN_HIDDEN:],
               ffn_w_down[1].astype(BF16), row2(final_norm_w), final_norm=True)
    return out.reshape(batch, n_real, D_MODEL)
```

```python
import functools
import math

import numpy as np
import jax
import jax.numpy as jnp
from jax import lax
from jax.experimental import pallas as pl
from jax.experimental.pallas import tpu as pltpu

F32 = jnp.float32
BF16 = jnp.bfloat16
HIGHEST = lax.Precision.HIGHEST

D_MODEL = 1024
N_META = 16
EPS = 1e-6
NEG_INF = -1e30
SSM_HEADDIM = 64
SSM_HEADS = 32
SSM_GROUPS = 8
SSM_HPG = SSM_HEADS // SSM_GROUPS
SSM_STATE = 128
SSM_CONV = 4
D_INNER = SSM_HEADS * SSM_HEADDIM
GROUP_W = D_INNER // SSM_GROUPS
CONV_DIM = D_INNER + 2 * SSM_GROUPS * SSM_STATE
DIFF_HEADS = 8
DIFF_HEAD_DIM = 64
DIFF_VDIM = 2 * DIFF_HEAD_DIM
REL_BUCKETS = 32
REL_MAX_DIST = 128
FFN_HIDDEN = 2816

SUBLANES = 8
LANES = 128
CHUNK = 256
ROW_TILE = 512
MXU_N = 256
TQ = 512
TK = 512
BAND = REL_MAX_DIST
META_KEYS = 128
V_ROWS = DIFF_VDIM + 16
LOG2E = math.log2(math.e)
FFN_TILE = 1408
VMEM_LIMIT = 56 * 1024 * 1024


def _resident(shape):
    nd = len(shape)
    return pl.BlockSpec(shape, lambda *_: (0,) * nd, pipeline_mode=pl.Buffered(1))


def _params(*sem):
    return pltpu.CompilerParams(dimension_semantics=sem, vmem_limit_bytes=VMEM_LIMIT)


def _rmsnorm(x, w):
    return x * lax.rsqrt(jnp.mean(x * x, axis=-1, keepdims=True) + EPS) * w


def _softplus(v):
    return jnp.maximum(v, 0.0) + jnp.log(1.0 + jnp.exp(-jnp.abs(v)))


def _silu(v):
    return v * jax.nn.sigmoid(v)


_NT = (((1,), (1,)), ((), ()))


def _in_proj_kernel(x_ref, nw_ref, w_ref, wdt_ref, wdtT_ref, z_ref, xbc_ref, dt_ref, dtT_ref):
    xn = _rmsnorm(x_ref[...], nw_ref[...])
    xb = xn.astype(BF16)
    for c0 in range(0, D_INNER, 1024):
        z_ref[:, c0:c0 + 1024] = jnp.dot(
            xb, w_ref[:, c0:c0 + 1024], preferred_element_type=F32).astype(BF16)
    for c0 in range(0, CONV_DIM, 1024):
        xbc_ref[:, c0:c0 + 1024] = jnp.dot(
            xb, w_ref[:, D_INNER + c0:D_INNER + c0 + 1024], preferred_element_type=F32).astype(BF16)
    dt_ref[...] = jnp.dot(xn, wdt_ref[...], precision=HIGHEST, preferred_element_type=F32)
    dtT_ref[...] = lax.dot_general(wdtT_ref[...], xn, _NT, precision=HIGHEST,
                                   preferred_element_type=F32)


def _in_proj(h2d, nw, w_zx, w_dt, w_dtT):
    rows = h2d.shape[0]
    grid = (rows // ROW_TILE,)
    return pl.pallas_call(
        _in_proj_kernel,
        grid=grid,
        in_specs=[
            pl.BlockSpec((ROW_TILE, D_MODEL), lambda i: (i, 0)),
            _resident((1, D_MODEL)),
            _resident((D_MODEL, D_INNER + CONV_DIM)),
            _resident((D_MODEL, SSM_HEADS)),
            _resident((SSM_HEADS, D_MODEL)),
        ],
        out_specs=[
            pl.BlockSpec((ROW_TILE, D_INNER), lambda i: (i, 0)),
            pl.BlockSpec((ROW_TILE, CONV_DIM), lambda i: (i, 0)),
            pl.BlockSpec((ROW_TILE, SSM_HEADS), lambda i: (i, 0)),
            pl.BlockSpec((SSM_HEADS, ROW_TILE), lambda i: (0, i)),
        ],
        out_shape=[
            jax.ShapeDtypeStruct((rows, D_INNER), BF16),
            jax.ShapeDtypeStruct((rows, CONV_DIM), BF16),
            jax.ShapeDtypeStruct((rows, SSM_HEADS), F32),
            jax.ShapeDtypeStruct((SSM_HEADS, rows), F32),
        ],
        compiler_params=_params("parallel"),
        name="in_proj",
    )(h2d, nw, w_zx, w_dt, w_dtT)


def _ssd_kernel(z_ref, xbc_ref, dt_ref, dtT_ref, cw_ref, cb_ref, dtb_ref, dtbT_ref, alog_ref,
                alogT_ref, dsk_ref, nw_ref, y_ref, state_ref, xe_ref, xc_ref, *, n_pad):
    Q = CHUNK
    c = pl.program_id(1)

    @pl.when(c == 0)
    def _():
        state_ref[...] = jnp.zeros_like(state_ref)
        xe_ref[0:SUBLANES, :] = jnp.zeros((SUBLANES, CONV_DIM), F32)

    xe_ref[SUBLANES:SUBLANES + Q, :] = xbc_ref[...].astype(F32)

    def conv_cols(i, carry):
        cs = pl.ds(pl.multiple_of(i * 512, 512), 512)
        acc = cb_ref[:, cs] + cw_ref[3:4, cs] * xe_ref[SUBLANES:SUBLANES + Q, cs]
        for k in range(SSM_CONV - 1):
            off = SUBLANES - (SSM_CONV - 1) + k
            acc = acc + cw_ref[k:k + 1, cs] * xe_ref[off:off + Q, cs]
        xc_ref[:, cs] = _silu(acc)
        return carry

    lax.fori_loop(0, CONV_DIM // 512, conv_cols, 0)
    xe_ref[0:SUBLANES, :] = xe_ref[Q:Q + SUBLANES, :]

    first_valid = jnp.where(c == 0, n_pad, 0)
    row = lax.broadcasted_iota(jnp.int32, (Q, 1), 0)
    col = lax.broadcasted_iota(jnp.int32, (1, Q), 1)
    dtc = jnp.where(row >= first_valid, _softplus(dt_ref[...] + dtb_ref[...]), 0.0)
    dtr = jnp.where(col >= first_valid, _softplus(dtT_ref[...] + dtbT_ref[...]), 0.0)
    dac = dtc * (-jnp.exp(alog_ref[...]))
    dar = dtr * (-jnp.exp(alogT_ref[...]))
    ii = lax.broadcasted_iota(jnp.int32, (Q, Q), 0)
    jj = lax.broadcasted_iota(jnp.int32, (Q, Q), 1)
    tri = ii >= jj
    acs_c = jnp.dot(tri.astype(F32), dac, precision=HIGHEST, preferred_element_type=F32)
    acs_r = jnp.dot(dar, (ii <= jj).astype(F32), precision=HIGHEST, preferred_element_type=F32)
    eacs_c = jnp.exp(acs_c)
    lane = lax.broadcasted_iota(jnp.int32, (1, GROUP_W), 1)

    for g in range(SSM_GROUPS):
        b_g = xc_ref[:, D_INNER + g * SSM_STATE:D_INNER + (g + 1) * SSM_STATE]
        c_g = xc_ref[:, D_INNER + (SSM_GROUPS + g) * SSM_STATE:D_INNER + (SSM_GROUPS + g + 1) * SSM_STATE]
        x_g = xc_ref[:, g * GROUP_W:(g + 1) * GROUP_W]
        c_b = c_g.astype(BF16)
        cb = lax.dot_general(c_b, b_g.astype(BF16), _NT, preferred_element_type=F32)
        b_t = b_g.T
        s_old = state_ref[g]
        y_off = jnp.dot(c_b, s_old.astype(BF16), preferred_element_type=F32)
        y_diag = jnp.zeros((Q, GROUP_W), F32)
        s_add = jnp.zeros((SSM_STATE, GROUP_W), F32)
        scale = jnp.zeros((Q, GROUP_W), F32)
        sdec = jnp.zeros((1, GROUP_W), F32)
        for r in range(SSM_HPG):
            h = g * SSM_HPG + r
            ac = acs_c[:, h:h + 1]
            ar = acs_r[h:h + 1, :]
            dt_h = dtr[h:h + 1, :]
            decay = jnp.exp(jnp.where(tri, ac - ar, NEG_INF))
            w = (cb * decay * dt_h).astype(BF16)
            in_head = (lane >= r * SSM_HEADDIM) & (lane < (r + 1) * SSM_HEADDIM)
            x_r = jnp.where(in_head, x_g, 0.0).astype(BF16)
            y_diag = y_diag + jnp.dot(w, x_r, preferred_element_type=F32)
            a_last = ar[:, Q - 1:Q]
            w_state = jnp.exp(a_last - ar) * dt_h
            s_add = s_add + jnp.dot((b_t * w_state).astype(BF16), x_r, preferred_element_type=F32)
            scale = jnp.where(in_head, eacs_c[:, h:h + 1], scale)
            sdec = jnp.where(in_head, jnp.exp(a_last), sdec)
        state_ref[g] = s_old * sdec + s_add
        gs = slice(g * GROUP_W, (g + 1) * GROUP_W)
        y = y_diag + y_off * scale + x_g * dsk_ref[:, gs]
        y = y * _silu(z_ref[:, gs].astype(F32))
        y_ref[:, gs] = _rmsnorm(y, nw_ref[:, gs]).astype(BF16)


def _ssd(z, xbc, dt, dtT, cw, cb, dtb, dtbT, alog, alogT, dsk, nw, *, batch, n_chunks, n_pad):
    rows = z.shape[0]
    blk = lambda b, c: b * n_chunks + (c + n_chunks - 1) % n_chunks
    return pl.pallas_call(
        functools.partial(_ssd_kernel, n_pad=n_pad),
        grid=(batch, n_chunks),
        in_specs=[
            pl.BlockSpec((CHUNK, D_INNER), lambda b, c: (blk(b, c), 0)),
            pl.BlockSpec((CHUNK, CONV_DIM), lambda b, c: (blk(b, c), 0)),
            pl.BlockSpec((CHUNK, SSM_HEADS), lambda b, c: (blk(b, c), 0)),
            pl.BlockSpec((SSM_HEADS, CHUNK), lambda b, c: (0, blk(b, c))),
            _resident((SSM_CONV, CONV_DIM)),
            _resident((1, CONV_DIM)),
            _resident((1, SSM_HEADS)),
            _resident((SSM_HEADS, 1)),
            _resident((1, SSM_HEADS)),
            _resident((SSM_HEADS, 1)),
            _resident((1, D_INNER)),
            _resident((1, D_INNER)),
        ],
        out_specs=pl.BlockSpec((CHUNK, D_INNER), lambda b, c: (blk(b, c), 0)),
        out_shape=jax.ShapeDtypeStruct((rows, D_INNER), BF16),
        scratch_shapes=[
            pltpu.VMEM((SSM_GROUPS, SSM_STATE, GROUP_W), F32),
            pltpu.VMEM((CHUNK + 2 * SUBLANES, CONV_DIM), F32),
            pltpu.VMEM((CHUNK, CONV_DIM), F32),
        ],
        compiler_params=_params("arbitrary", "arbitrary"),
        name="ssd_scan",
    )(z, xbc, dt, dtT, cw, cb, dtb, dtbT, alog, alogT, dsk, nw)


def _mm_res_kernel(a_ref, w_ref, r_ref, o_ref):
    o_ref[...] = r_ref[...] + jnp.dot(a_ref[...], w_ref[...], preferred_element_type=F32)


def _mm_res(a, w, res, *, rows):
    k = a.shape[1]
    return pl.pallas_call(
        _mm_res_kernel,
        grid=(rows // ROW_TILE,),
        in_specs=[
            pl.BlockSpec((ROW_TILE, k), lambda i: (i, 0)),
            _resident((k, D_MODEL)),
            pl.BlockSpec((ROW_TILE, D_MODEL), lambda i: (i, 0)),
        ],
        out_specs=pl.BlockSpec((ROW_TILE, D_MODEL), lambda i: (i, 0)),
        out_shape=jax.ShapeDtypeStruct((rows, D_MODEL), F32),
        compiler_params=_params("parallel"),
        name="proj_residual",
    )(a, w, res)


def _ffn_kernel(h_ref, nw_ref, wg_ref, wv_ref, wd_ref, fnw_ref, o_ref, *, final_norm):
    x = h_ref[...]
    xb = _rmsnorm(x, nw_ref[...]).astype(BF16)
    acc = x
    for t0 in range(0, FFN_HIDDEN, FFN_TILE):
        ts = slice(t0, t0 + FFN_TILE)
        gate = jnp.dot(xb, wg_ref[:, ts], preferred_element_type=F32)
        val = jnp.dot(xb, wv_ref[:, ts], preferred_element_type=F32)
        act = (_silu(gate) * val).astype(BF16)
        acc = acc + jnp.dot(act, wd_ref[ts, :], preferred_element_type=F32)
    if final_norm:
        acc = _rmsnorm(acc, fnw_ref[...])
    o_ref[...] = acc


def _ffn(h, nw, wg, wv, wd, fnw, *, final_norm):
    rows = h.shape[0]
    return pl.pallas_call(
        functools.partial(_ffn_kernel, final_norm=final_norm),
        grid=(rows // ROW_TILE,),
        in_specs=[
            pl.BlockSpec((ROW_TILE, D_MODEL), lambda i: (i, 0)),
            _resident((1, D_MODEL)),
            _resident((D_MODEL, FFN_HIDDEN)),
            _resident((D_MODEL, FFN_HIDDEN)),
            _resident((FFN_HIDDEN, D_MODEL)),
            _resident((1, D_MODEL)),
        ],
        out_specs=pl.BlockSpec((ROW_TILE, D_MODEL), lambda i: (i, 0)),
        out_shape=jax.ShapeDtypeStruct((rows, D_MODEL), F32),
        compiler_params=_params("parallel"),
        name="ffn",
    )(h, nw, wg, wv, wd, fnw)


def _kvq_kernel(h_ref, kvnw_ref, qnw_ref, wk_ref, wvT_ref, wqT_ref, k_ref, vT_ref, qT_ref):
    x = h_ref[...]
    inv = lax.rsqrt(jnp.mean(x * x, axis=-1, keepdims=True) + EPS)
    xkv = (x * inv * kvnw_ref[...]).astype(BF16)
    xq = (x * inv * qnw_ref[...]).astype(BF16)
    k_ref[...] = jnp.dot(xkv, wk_ref[...], preferred_element_type=F32).astype(BF16)
    vT_ref[0] = lax.dot_general(wvT_ref[...], xkv, _NT, preferred_element_type=F32).astype(BF16)
    qT = lax.dot_general(wqT_ref[...], xq, _NT, preferred_element_type=F32)
    qT_ref[0] = (qT * (DIFF_HEAD_DIM ** -0.5 * LOG2E)).astype(BF16)


def _kvq(h, kvnw, qnw, wk, wvT, wqT, *, batch, lp):
    tiles = lp // CHUNK
    return pl.pallas_call(
        _kvq_kernel,
        grid=(batch, tiles),
        in_specs=[
            pl.BlockSpec((CHUNK, D_MODEL), lambda b, i: (b * tiles + i, 0)),
            _resident((1, D_MODEL)),
            _resident((1, D_MODEL)),
            _resident((D_MODEL, D_MODEL)),
            _resident((D_MODEL, D_MODEL)),
            _resident((D_MODEL, D_MODEL)),
        ],
        out_specs=[
            pl.BlockSpec((CHUNK, D_MODEL), lambda b, i: (b * tiles + i, 0)),
            pl.BlockSpec((1, D_MODEL, CHUNK), lambda b, i: (b, 0, i)),
            pl.BlockSpec((1, D_MODEL, CHUNK), lambda b, i: (b, 0, i)),
        ],
        out_shape=[
            jax.ShapeDtypeStruct((batch * lp, D_MODEL), BF16),
            jax.ShapeDtypeStruct((batch, D_MODEL, lp), BF16),
            jax.ShapeDtypeStruct((batch, D_MODEL, lp), BF16),
        ],
        compiler_params=_params("parallel", "parallel"),
        name="kvq_proj",
    )(h, kvnw, qnw, wk, wvT, wqT)


def _rel_bucket_np(n):
    n = np.asarray(n)
    max_exact = REL_BUCKETS // 2
    nf = np.maximum(n, 1).astype(np.float32)
    large = max_exact + (np.log(nf / np.float32(max_exact)) / np.float32(math.log(REL_MAX_DIST / max_exact))
                         * np.float32(REL_BUCKETS - max_exact)).astype(np.int32)
    large = np.minimum(large, REL_BUCKETS - 1)
    return np.where(n < max_exact, n, large).astype(np.int32)


def _bucket_tiles():
    i = np.arange(TK)[:, None]
    j = np.arange(TQ)[None, :]
    diag = np.where(j - i >= 0, _rel_bucket_np(np.maximum(j - i, 0)), -1)
    a = np.arange(BAND)[:, None]
    b = np.arange(BAND)[None, :]
    corner = _rel_bucket_np(b - a + BAND)
    m = np.arange(META_KEYS)[:, None] - (META_KEYS - N_META)
    dist0 = N_META + j - m
    meta0 = np.where(m >= 0, _rel_bucket_np(np.maximum(dist0, 0)), -1)
    meta_far = np.where(m >= 0, REL_BUCKETS - 1, -1) + 0 * j
    return (corner.astype(np.int32), diag.astype(np.int32),
            np.stack([meta0, meta_far]).astype(np.int32))


def _bias_kernel(tab_ref, corner_id_ref, diag_id_ref, meta_id_ref, corner_ref, diag_ref, meta_ref):
    h = pl.program_id(0)
    far = tab_ref[REL_BUCKETS - 1, h]

    def build(ids):
        out = jnp.where(ids < 0, NEG_INF, 0.0).astype(F32)
        for b in range(REL_BUCKETS - 1):
            out = jnp.where(ids == b, (tab_ref[b, h] - far) * LOG2E, out)
        return out

    corner_ref[0] = build(corner_id_ref[...])
    diag_ref[0] = build(diag_id_ref[...])
    meta_ref[0] = build(meta_id_ref[...])


def _bias_tiles(rel_bias):
    corner_ids, diag_ids, meta_ids = _bucket_tiles()
    return pl.pallas_call(
        _bias_kernel,
        grid=(DIFF_HEADS,),
        in_specs=[
            pl.BlockSpec(memory_space=pltpu.SMEM),
            _resident((BAND, BAND)),
            _resident((TK, TQ)),
            _resident((2, META_KEYS, TQ)),
        ],
        out_specs=[
            pl.BlockSpec((1, BAND, BAND), lambda h: (h, 0, 0)),
            pl.BlockSpec((1, TK, TQ), lambda h: (h, 0, 0)),
            pl.BlockSpec((1, 2, META_KEYS, TQ), lambda h: (h, 0, 0, 0)),
        ],
        out_shape=[
            jax.ShapeDtypeStruct((DIFF_HEADS, BAND, BAND), F32),
            jax.ShapeDtypeStruct((DIFF_HEADS, TK, TQ), F32),
            jax.ShapeDtypeStruct((DIFF_HEADS, 2, META_KEYS, TQ), F32),
        ],
        compiler_params=_params("arbitrary"),
        name="rel_bias_tiles",
    )(rel_bias, jnp.asarray(corner_ids), jnp.asarray(diag_ids), jnp.asarray(meta_ids))


def _attn_kernel(qT_ref, k_ref, vT_ref, corner_ref, diag_ref, meta_ref, lam_ref, sw_ref, o_ref,
                 s_ref, p_ref, alpha_ref, m_ref, acc_ref, *, n_real, lambda_init):
    qi = pl.program_id(2)
    qT = qT_ref[0]
    sub = lax.broadcasted_iota(jnp.int32, (DIFF_VDIM, 1), 0)
    q_maps = (jnp.where(sub < DIFF_HEAD_DIM, qT, jnp.zeros_like(qT)),
              jnp.where(sub >= DIFF_HEAD_DIM, qT, jnp.zeros_like(qT)))
    chains = [(c, slice(qh * MXU_N, (qh + 1) * MXU_N)) for c in range(2) for qh in range(TQ // MXU_N)]

    def v_ext(lo, n):
        return jnp.concatenate([vT_ref[0, :, pl.ds(lo, n)],
                                jnp.ones((V_ROWS - DIFF_VDIM, n), BF16)], axis=0)

    def key_lo(step_idx):
        if isinstance(step_idx, int):
            return max(step_idx, 0) * TK
        return pl.multiple_of(jnp.maximum(step_idx, 0) * TK, TK)

    def scores_into(slot, lo):
        k_blk = k_ref[pl.ds(lo, TK), :]
        for c, qs in chains:
            s_ref[slot, c, :, qs] = jnp.dot(k_blk, q_maps[c][:, qs], preferred_element_type=F32)

    def accumulate(slot, lo):
        vx = v_ext(lo, TK)
        for c, qs in chains:
            acc_ref[c, :, qs] = (alpha_ref[slot, c, :, qs] * acc_ref[c, :, qs]
                                 + jnp.dot(vx, p_ref[slot, c, :, qs], preferred_element_type=F32))

    def step(slot, idx, bias_ref, lag):
        scores_into(1 - slot, key_lo(idx - 1))
        for c, qs in chains:
            s = s_ref[slot, c, :, qs]
            if bias_ref is not None:
                s = s + bias_ref[0, :, qs]
            m_old = m_ref[c, :, qs]
            m_new = jnp.maximum(m_old, jnp.max(s, axis=0, keepdims=True))
            alpha_ref[slot, c, :, qs] = jnp.exp2(m_old - m_new)
            p_ref[slot, c, :, qs] = jnp.exp2(s - m_new).astype(BF16)
            m_ref[c, :, qs] = m_new
        if lag:
            accumulate(1 - slot, key_lo(idx + 1))

    meta_lo = n_real + CHUNK - META_KEYS
    k_m = k_ref[meta_lo:meta_lo + META_KEYS, :]
    vx_m = v_ext(meta_lo, META_KEYS)
    bias_m = meta_ref[0, jnp.minimum(qi, 1)]
    for c in range(2):
        s = jnp.dot(k_m, q_maps[c], preferred_element_type=F32) + bias_m
        m_c = jnp.max(s, axis=0, keepdims=True)
        m_ref[c] = m_c
        acc_ref[c] = jnp.dot(vx_m, jnp.exp2(s - m_c).astype(BF16), preferred_element_type=F32)

    scores_into(0, key_lo(qi))
    step(0, qi, diag_ref, lag=False)

    @pl.when(qi > 0)
    def _():
        for c in range(2):
            s_ref[1, c, TK - BAND:TK, 0:BAND] = s_ref[1, c, TK - BAND:TK, 0:BAND] + corner_ref[0]
        step(1, qi - 1, None, lag=True)

        def far_pair(i, carry):
            idx = qi - 2 - 2 * i
            step(0, idx, None, lag=True)
            step(1, idx - 1, None, lag=True)
            return carry

        n_far = qi - 1
        lax.fori_loop(0, n_far // 2, far_pair, 0)
        pl.when(n_far % 2 == 1)(lambda: step(0, 0, None, lag=True))

    pl.when(qi % 2 == 0)(lambda: accumulate(0, key_lo(0)))
    pl.when(qi % 2 == 1)(lambda: accumulate(1, key_lo(0)))

    lv = lam_ref[...]
    lam = (jnp.exp(jnp.sum(lv[0:1] * lv[1:2], axis=-1, keepdims=True))
           - jnp.exp(jnp.sum(lv[2:3] * lv[3:4], axis=-1, keepdims=True)) + lambda_init)
    o = [acc_ref[c, 0:DIFF_VDIM, :] / acc_ref[c, DIFF_VDIM:DIFF_VDIM + 1, :] for c in range(2)]
    o = o[0] - lam * o[1]
    o = o * lax.rsqrt(jnp.mean(o * o, axis=0, keepdims=True) + EPS)
    o = o * sw_ref[...] * (1.0 - lambda_init)
    o_ref[...] = o.T.astype(BF16)


def _attention(qT, k, vT, corner, diag, meta, lamv, sw, *, batch, n_real, lp, lambda_init):
    nq = n_real // TQ
    return pl.pallas_call(
        functools.partial(_attn_kernel, n_real=n_real, lambda_init=lambda_init),
        grid=(batch, DIFF_HEADS, nq),
        in_specs=[
            pl.BlockSpec((1, DIFF_VDIM, TQ), lambda b, h, i: (b, h, i)),
            pl.BlockSpec((lp, DIFF_VDIM), lambda b, h, i: (b, h)),
            pl.BlockSpec((1, DIFF_VDIM, lp), lambda b, h, i: (b, h, 0)),
            pl.BlockSpec((1, BAND, BAND), lambda b, h, i: (h, 0, 0)),
            pl.BlockSpec((1, TK, TQ), lambda b, h, i: (h, 0, 0)),
            pl.BlockSpec((1, 2, META_KEYS, TQ), lambda b, h, i: (h, 0, 0, 0)),
            _resident((4, DIFF_HEAD_DIM)),
            _resident((DIFF_VDIM, 1)),
        ],
        out_specs=pl.BlockSpec((TQ, DIFF_VDIM), lambda b, h, i: (b * nq + i, h)),
        out_shape=jax.ShapeDtypeStruct((batch * n_real, D_MODEL), BF16),
        scratch_shapes=[
            pltpu.VMEM((2, 2, TK, TQ), F32),
            pltpu.VMEM((2, 2, TK, TQ), BF16),
            pltpu.VMEM((2, 2, 1, TQ), F32),
            pltpu.VMEM((2, 1, TQ), F32),
            pltpu.VMEM((2, V_ROWS, TQ), F32),
        ],
        compiler_params=_params("parallel", "parallel", "arbitrary"),
        name="diff_attention",
    )(qT, k, vT, corner, diag, meta, lamv, sw)


def _mm_res3_kernel(a_ref, w_ref, r_ref, o_ref):
    o_ref[...] = r_ref[0] + jnp.dot(a_ref[...], w_ref[...], preferred_element_type=F32)


def _mm_res_real(a, w, res3, *, batch, n_real):
    tiles = n_real // ROW_TILE
    k = a.shape[1]
    return pl.pallas_call(
        _mm_res3_kernel,
        grid=(batch, tiles),
        in_specs=[
            pl.BlockSpec((ROW_TILE, k), lambda b, i: (b * tiles + i, 0)),
            _resident((k, D_MODEL)),
            pl.BlockSpec((1, ROW_TILE, D_MODEL), lambda b, i: (b, i, 0)),
        ],
        out_specs=pl.BlockSpec((ROW_TILE, D_MODEL), lambda b, i: (b * tiles + i, 0)),
        out_shape=jax.ShapeDtypeStruct((batch * n_real, D_MODEL), F32),
        compiler_params=_params("parallel", "parallel"),
        name="attn_out_residual",
    )(a, w, res3)


def kernel(x, meta_tokens, norm_w, ssm_in_w, ssm_conv_w, ssm_conv_b, ssm_dt_bias, ssm_a_log, ssm_d,
           ssm_norm_w, ssm_out_w, kv_norm_w, w_kv, w_q, lam_q1, lam_k1, lam_q2, lam_k2, subln_w,
           w_o, rel_bias, ffn_w_gu, ffn_w_down, final_norm_w):
    batch, n_real, d = x.shape
    assert d == D_MODEL and n_real % TQ == 0 and n_real % ROW_TILE == 0 and TQ == TK
    assert norm_w.shape[0] == 2 and ssm_in_w.shape[0] == 1 and w_q.shape[0] == 1
    lp = n_real + CHUNK
    n_pad = CHUNK - N_META
    n_chunks = lp // CHUNK
    rows = batch * lp
    assert rows % ROW_TILE == 0
    row2 = lambda v: v.reshape(1, -1).astype(F32)

    tail = jnp.concatenate([jnp.zeros((n_pad, D_MODEL), F32), meta_tokens.astype(F32)], axis=0)
    h0 = jnp.concatenate([x, jnp.broadcast_to(tail[None], (batch, CHUNK, D_MODEL))], axis=1)
    h0 = h0.reshape(rows, D_MODEL)

    in_w = ssm_in_w[0]
    w_zx = in_w[:, :D_INNER + CONV_DIM].astype(BF16)
    w_dt = in_w[:, D_INNER + CONV_DIM:]
    z, xbc, dt, dtT = _in_proj(h0, row2(norm_w[0, 0]), w_zx, w_dt, w_dt.T)
    y = _ssd(z, xbc, dt, dtT, ssm_conv_w[0], row2(ssm_conv_b[0]),
             row2(ssm_dt_bias[0]), ssm_dt_bias[0].reshape(-1, 1),
             row2(ssm_a_log[0]), ssm_a_log[0].reshape(-1, 1),
             row2(jnp.repeat(ssm_d[0], SSM_HEADDIM)), row2(ssm_norm_w[0]),
             batch=batch, n_chunks=n_chunks, n_pad=n_pad)
    h1 = _mm_res(y, ssm_out_w[0].astype(BF16), h0, rows=rows)
    gu0 = ffn_w_gu[0].astype(BF16)
    h2 = _ffn(h1, row2(norm_w[0, 1]), gu0[:, :FFN_HIDDEN], gu0[:, FFN_HIDDEN:],
              ffn_w_down[0].astype(BF16), row2(final_norm_w), final_norm=False)

    layer = 1
    lambda_init = 0.8 - 0.6 * math.exp(-0.3 * layer)
    k, vT, qT = _kvq(h2, row2(kv_norm_w), row2(norm_w[1, 0]), w_kv[:, :D_MODEL].astype(BF16),
                     w_kv[:, D_MODEL:].T.astype(BF16), w_q[0].T.astype(BF16), batch=batch, lp=lp)
    corner, diag, meta = _bias_tiles(rel_bias.astype(F32))
    lamv = jnp.stack([lam_q1[0], lam_k1[0], lam_q2[0], lam_k2[0]]).astype(F32)
    attn = _attention(qT, k, vT, corner, diag, meta, lamv, subln_w[0].reshape(-1, 1).astype(F32),
                      batch=batch, n_real=n_real, lp=lp, lambda_init=lambda_init)
    h3 = _mm_res_real(attn, w_o[0].astype(BF16), h2.reshape(batch, lp, D_MODEL),
                      batch=batch, n_real=n_real)
    gu1 = ffn_w_gu[1].astype(BF16)
    out = _ffn(h3, row2(norm_w[1, 1]), gu1[:, :FFN_HIDDEN], gu1[:, FFN_HIDDEN:],
               ffn_w_down[1].astype(BF16), row2(final_norm_w), final_norm=True)
    return out.reshape(batch, n_real, D_MODEL)
```

```python
import functools
import math

import numpy as np
import jax
import jax.numpy as jnp
from jax import lax
from jax.experimental import pallas as pl
from jax.experimental.pallas import tpu as pltpu

F32 = jnp.float32
BF16 = jnp.bfloat16
HIGHEST = lax.Precision.HIGHEST

D_MODEL = 1024
N_META = 16
EPS = 1e-6
NEG_INF = -1e30
SSM_HEADDIM = 64
SSM_HEADS = 32
SSM_GROUPS = 8
SSM_HPG = SSM_HEADS // SSM_GROUPS
SSM_STATE = 128
SSM_CONV = 4
D_INNER = SSM_HEADS * SSM_HEADDIM
GROUP_W = D_INNER // SSM_GROUPS
CONV_DIM = D_INNER + 2 * SSM_GROUPS * SSM_STATE
DIFF_HEADS = 8
DIFF_HEAD_DIM = 64
DIFF_VDIM = 2 * DIFF_HEAD_DIM
REL_BUCKETS = 32
REL_MAX_DIST = 128
FFN_HIDDEN = 2816

SUBLANES = 8
LANES = 128
CHUNK = 256
ROW_TILE = 512
CONV_TILE = 512
MXU_N = 256
TQ = 1024
TK = 512
BAND = REL_MAX_DIST
META_KEYS = 128
V_ROWS = DIFF_VDIM + 16
LOG2E = math.log2(math.e)
FFN_TILE = 1408
VMEM_LIMIT = 56 * 1024 * 1024


def _resident(shape):
    nd = len(shape)
    return pl.BlockSpec(shape, lambda *_: (0,) * nd, pipeline_mode=pl.Buffered(1))


def _params(*sem):
    return pltpu.CompilerParams(dimension_semantics=sem, vmem_limit_bytes=VMEM_LIMIT)


def _rmsnorm(x, w):
    return x * lax.rsqrt(jnp.mean(x * x, axis=-1, keepdims=True) + EPS) * w


def _softplus(v):
    return jnp.maximum(v, 0.0) + jnp.log(1.0 + jnp.exp(-jnp.abs(v)))


def _silu(v):
    return v * jax.nn.sigmoid(v)


_NT = (((1,), (1,)), ((), ()))


def _in_proj_kernel(x_ref, nw_ref, w_ref, wdt_ref, wdtT_ref, cw_ref, cb_ref,
                    z_ref, xc_ref, dt_ref, dtT_ref, xe_ref):
    Q = CHUNK
    c = pl.program_id(1)

    @pl.when(c == 0)
    def _():
        xe_ref[0:SUBLANES, :] = jnp.zeros((SUBLANES, CONV_DIM), F32)

    xn = _rmsnorm(x_ref[...], nw_ref[...])
    xb = xn.astype(BF16)
    for c0 in range(0, D_INNER, 1024):
        z_ref[:, c0:c0 + 1024] = jnp.dot(
            xb, w_ref[:, c0:c0 + 1024], preferred_element_type=F32).astype(BF16)
    for c0 in range(0, CONV_DIM, CONV_TILE):
        cs = slice(c0, c0 + CONV_TILE)
        pre = jnp.dot(xb, w_ref[:, D_INNER + c0:D_INNER + c0 + CONV_TILE], preferred_element_type=F32)
        xe_ref[SUBLANES:SUBLANES + Q, cs] = pre
        acc = cb_ref[:, cs] + cw_ref[SSM_CONV - 1:SSM_CONV, cs] * pre
        for k in range(SSM_CONV - 1):
            off = SUBLANES - (SSM_CONV - 1) + k
            acc = acc + cw_ref[k:k + 1, cs] * xe_ref[off:off + Q, cs]
        xc_ref[:, cs] = _silu(acc).astype(BF16)
    xe_ref[0:SUBLANES, :] = xe_ref[Q:Q + SUBLANES, :]
    dt_ref[...] = jnp.dot(xn, wdt_ref[...], precision=HIGHEST, preferred_element_type=F32)
    dtT_ref[...] = lax.dot_general(wdtT_ref[...], xn, _NT, precision=HIGHEST,
                                   preferred_element_type=F32)


def _scan_block(n_chunks):
    return lambda b, c: b * n_chunks + (c + n_chunks - 1) % n_chunks


def _in_proj(h2d, nw, w_zx, w_dt, w_dtT, cw, cb, *, batch, n_chunks):
    rows = h2d.shape[0]
    blk = _scan_block(n_chunks)
    return pl.pallas_call(
        _in_proj_kernel,
        grid=(batch, n_chunks),
        in_specs=[
            pl.BlockSpec((CHUNK, D_MODEL), lambda b, c: (blk(b, c), 0)),
            _resident((1, D_MODEL)),
            _resident((D_MODEL, D_INNER + CONV_DIM)),
            _resident((D_MODEL, SSM_HEADS)),
            _resident((SSM_HEADS, D_MODEL)),
            _resident((SSM_CONV, CONV_DIM)),
            _resident((1, CONV_DIM)),
        ],
        out_specs=[
            pl.BlockSpec((CHUNK, D_INNER), lambda b, c: (blk(b, c), 0)),
            pl.BlockSpec((CHUNK, CONV_DIM), lambda b, c: (blk(b, c), 0)),
            pl.BlockSpec((CHUNK, SSM_HEADS), lambda b, c: (blk(b, c), 0)),
            pl.BlockSpec((SSM_HEADS, CHUNK), lambda b, c: (0, blk(b, c))),
        ],
        out_shape=[
            jax.ShapeDtypeStruct((rows, D_INNER), BF16),
            jax.ShapeDtypeStruct((rows, CONV_DIM), BF16),
            jax.ShapeDtypeStruct((rows, SSM_HEADS), F32),
            jax.ShapeDtypeStruct((SSM_HEADS, rows), F32),
        ],
        scratch_shapes=[pltpu.VMEM((CHUNK + 2 * SUBLANES, CONV_DIM), F32)],
        compiler_params=_params("arbitrary", "arbitrary"),
        name="in_proj",
    )(h2d, nw, w_zx, w_dt, w_dtT, cw, cb)


def _ssd_kernel(z_ref, xc_ref, dt_ref, dtT_ref, dtb_ref, dtbT_ref, alog_ref, alogT_ref,
                dsk_ref, nw_ref, y_ref, state_ref, *, n_pad):
    Q = CHUNK
    c = pl.program_id(1)

    @pl.when(c == 0)
    def _():
        state_ref[...] = jnp.zeros_like(state_ref)

    first_valid = jnp.where(c == 0, n_pad, 0)
    row = lax.broadcasted_iota(jnp.int32, (Q, 1), 0)
    col = lax.broadcasted_iota(jnp.int32, (1, Q), 1)
    dtc = jnp.where(row >= first_valid, _softplus(dt_ref[...] + dtb_ref[...]), 0.0)
    dtr = jnp.where(col >= first_valid, _softplus(dtT_ref[...] + dtbT_ref[...]), 0.0)
    dac = dtc * (-jnp.exp(alog_ref[...]))
    dar = dtr * (-jnp.exp(alogT_ref[...]))
    ii = lax.broadcasted_iota(jnp.int32, (Q, Q), 0)
    jj = lax.broadcasted_iota(jnp.int32, (Q, Q), 1)
    tri = ii >= jj
    ac2 = jnp.dot(tri.astype(F32), dac, precision=HIGHEST, preferred_element_type=F32) * LOG2E
    ar2 = jnp.dot(dar, (ii <= jj).astype(F32), precision=HIGHEST, preferred_element_type=F32) * LOG2E
    ar2_dt = ar2 - jnp.where(dtr > 0.0, jnp.log2(dtr), NEG_INF)
    a_last2 = ar2[:, Q - 1:Q]
    w_state = jnp.exp2(a_last2 - ar2_dt)
    e_last = jnp.exp2(a_last2)
    eacs_c = jnp.exp2(ac2)
    lane = lax.broadcasted_iota(jnp.int32, (1, GROUP_W), 1)

    for g in range(SSM_GROUPS):
        b_g = xc_ref[:, D_INNER + g * SSM_STATE:D_INNER + (g + 1) * SSM_STATE]
        c_g = xc_ref[:, D_INNER + (SSM_GROUPS + g) * SSM_STATE:D_INNER + (SSM_GROUPS + g + 1) * SSM_STATE]
        x_g = xc_ref[:, g * GROUP_W:(g + 1) * GROUP_W]
        cb = lax.dot_general(c_g, b_g, _NT, preferred_element_type=F32)
        b_t = b_g.astype(F32).T
        s_old = state_ref[g]
        y_off = jnp.dot(c_g, s_old.astype(BF16), preferred_element_type=F32)
        y_diag = jnp.zeros((Q, GROUP_W), F32)
        s_add = jnp.zeros((SSM_STATE, GROUP_W), F32)
        scale = jnp.zeros((Q, GROUP_W), F32)
        sdec = jnp.zeros((1, GROUP_W), F32)
        for r in range(SSM_HPG):
            h = g * SSM_HPG + r
            decay_dt = jnp.exp2(jnp.where(tri, ac2[:, h:h + 1] - ar2_dt[h:h + 1, :], NEG_INF))
            w = (cb * decay_dt).astype(BF16)
            in_head = (lane >= r * SSM_HEADDIM) & (lane < (r + 1) * SSM_HEADDIM)
            x_r = jnp.where(in_head, x_g, jnp.zeros_like(x_g))
            y_diag = y_diag + jnp.dot(w, x_r, preferred_element_type=F32)
            s_add = s_add + jnp.dot((b_t * w_state[h:h + 1, :]).astype(BF16), x_r,
                                    preferred_element_type=F32)
            scale = jnp.where(in_head, eacs_c[:, h:h + 1], scale)
            sdec = jnp.where(in_head, e_last[h:h + 1, :], sdec)
        state_ref[g] = s_old * sdec + s_add
        gs = slice(g * GROUP_W, (g + 1) * GROUP_W)
        y = y_diag + y_off * scale + x_g.astype(F32) * dsk_ref[:, gs]
        y = y * _silu(z_ref[:, gs].astype(F32))
        y_ref[:, gs] = _rmsnorm(y, nw_ref[:, gs]).astype(BF16)


def _ssd(z, xc, dt, dtT, dtb, dtbT, alog, alogT, dsk, nw, *, batch, n_chunks, n_pad):
    rows = z.shape[0]
    blk = _scan_block(n_chunks)
    return pl.pallas_call(
        functools.partial(_ssd_kernel, n_pad=n_pad),
        grid=(batch, n_chunks),
        in_specs=[
            pl.BlockSpec((CHUNK, D_INNER), lambda b, c: (blk(b, c), 0)),
            pl.BlockSpec((CHUNK, CONV_DIM), lambda b, c: (blk(b, c), 0)),
            pl.BlockSpec((CHUNK, SSM_HEADS), lambda b, c: (blk(b, c), 0)),
            pl.BlockSpec((SSM_HEADS, CHUNK), lambda b, c: (0, blk(b, c))),
            _resident((1, SSM_HEADS)),
            _resident((SSM_HEADS, 1)),
            _resident((1, SSM_HEADS)),
            _resident((SSM_HEADS, 1)),
            _resident((1, D_INNER)),
            _resident((1, D_INNER)),
        ],
        out_specs=pl.BlockSpec((CHUNK, D_INNER), lambda b, c: (blk(b, c), 0)),
        out_shape=jax.ShapeDtypeStruct((rows, D_INNER), BF16),
        scratch_shapes=[pltpu.VMEM((SSM_GROUPS, SSM_STATE, GROUP_W), F32)],
        compiler_params=_params("arbitrary", "arbitrary"),
        name="ssd_scan",
    )(z, xc, dt, dtT, dtb, dtbT, alog, alogT, dsk, nw)


def _mm_res_kernel(a_ref, w_ref, r_ref, o_ref):
    o_ref[...] = r_ref[...] + jnp.dot(a_ref[...], w_ref[...], preferred_element_type=F32)


def _mm_res(a, w, res, *, rows):
    k = a.shape[1]
    return pl.pallas_call(
        _mm_res_kernel,
        grid=(rows // ROW_TILE,),
        in_specs=[
            pl.BlockSpec((ROW_TILE, k), lambda i: (i, 0)),
            _resident((k, D_MODEL)),
            pl.BlockSpec((ROW_TILE, D_MODEL), lambda i: (i, 0)),
        ],
        out_specs=pl.BlockSpec((ROW_TILE, D_MODEL), lambda i: (i, 0)),
        out_shape=jax.ShapeDtypeStruct((rows, D_MODEL), F32),
        compiler_params=_params("parallel"),
        name="proj_residual",
    )(a, w, res)


def _ffn_kernel(h_ref, nw_ref, wg_ref, wv_ref, wd_ref, fnw_ref, o_ref, *, final_norm):
    x = h_ref[...]
    xb = _rmsnorm(x, nw_ref[...]).astype(BF16)
    acc = x
    for t0 in range(0, FFN_HIDDEN, FFN_TILE):
        ts = slice(t0, t0 + FFN_TILE)
        gate = jnp.dot(xb, wg_ref[:, ts], preferred_element_type=F32)
        val = jnp.dot(xb, wv_ref[:, ts], preferred_element_type=F32)
        act = (_silu(gate) * val).astype(BF16)
        acc = acc + jnp.dot(act, wd_ref[ts, :], preferred_element_type=F32)
    if final_norm:
        acc = _rmsnorm(acc, fnw_ref[...])
    o_ref[...] = acc


def _ffn(h, nw, wg, wv, wd, fnw, *, final_norm):
    rows = h.shape[0]
    return pl.pallas_call(
        functools.partial(_ffn_kernel, final_norm=final_norm),
        grid=(rows // ROW_TILE,),
        in_specs=[
            pl.BlockSpec((ROW_TILE, D_MODEL), lambda i: (i, 0)),
            _resident((1, D_MODEL)),
            _resident((D_MODEL, FFN_HIDDEN)),
            _resident((D_MODEL, FFN_HIDDEN)),
            _resident((FFN_HIDDEN, D_MODEL)),
            _resident((1, D_MODEL)),
        ],
        out_specs=pl.BlockSpec((ROW_TILE, D_MODEL), lambda i: (i, 0)),
        out_shape=jax.ShapeDtypeStruct((rows, D_MODEL), F32),
        compiler_params=_params("parallel"),
        name="ffn",
    )(h, nw, wg, wv, wd, fnw)


def _kvq_kernel(h_ref, kvnw_ref, qnw_ref, wk_ref, wvT_ref, wqT_ref, k_ref, vT_ref, qT_ref):
    x = h_ref[...]
    inv = lax.rsqrt(jnp.mean(x * x, axis=-1, keepdims=True) + EPS)
    xkv = (x * inv * kvnw_ref[...]).astype(BF16)
    xq = (x * inv * qnw_ref[...]).astype(BF16)
    k_ref[...] = jnp.dot(xkv, wk_ref[...], preferred_element_type=F32).astype(BF16)
    vT_ref[0] = lax.dot_general(wvT_ref[...], xkv, _NT, preferred_element_type=F32).astype(BF16)
    qT = lax.dot_general(wqT_ref[...], xq, _NT, preferred_element_type=F32)
    qT_ref[0] = (qT * (DIFF_HEAD_DIM ** -0.5 * LOG2E)).astype(BF16)


def _kvq(h, kvnw, qnw, wk, wvT, wqT, *, batch, lp):
    tiles = lp // CHUNK
    return pl.pallas_call(
        _kvq_kernel,
        grid=(batch, tiles),
        in_specs=[
            pl.BlockSpec((CHUNK, D_MODEL), lambda b, i: (b * tiles + i, 0)),
            _resident((1, D_MODEL)),
            _resident((1, D_MODEL)),
            _resident((D_MODEL, D_MODEL)),
            _resident((D_MODEL, D_MODEL)),
            _resident((D_MODEL, D_MODEL)),
        ],
        out_specs=[
            pl.BlockSpec((CHUNK, D_MODEL), lambda b, i: (b * tiles + i, 0)),
            pl.BlockSpec((1, D_MODEL, CHUNK), lambda b, i: (b, 0, i)),
            pl.BlockSpec((1, D_MODEL, CHUNK), lambda b, i: (b, 0, i)),
        ],
        out_shape=[
            jax.ShapeDtypeStruct((batch * lp, D_MODEL), BF16),
            jax.ShapeDtypeStruct((batch, D_MODEL, lp), BF16),
            jax.ShapeDtypeStruct((batch, D_MODEL, lp), BF16),
        ],
        compiler_params=_params("parallel", "parallel"),
        name="kvq_proj",
    )(h, kvnw, qnw, wk, wvT, wqT)


def _rel_bucket_np(n):
    n = np.asarray(n)
    max_exact = REL_BUCKETS // 2
    nf = np.maximum(n, 1).astype(np.float32)
    large = max_exact + (np.log(nf / np.float32(max_exact)) / np.float32(math.log(REL_MAX_DIST / max_exact))
                         * np.float32(REL_BUCKETS - max_exact)).astype(np.int32)
    large = np.minimum(large, REL_BUCKETS - 1)
    return np.where(n < max_exact, n, large).astype(np.int32)


def _bucket_tiles():
    i = np.arange(TK)[:, None]
    jd = np.arange(TK)[None, :]
    j = np.arange(TQ)[None, :]
    diag = np.where(jd - i >= 0, _rel_bucket_np(np.maximum(jd - i, 0)), -1)
    a = np.arange(BAND)[:, None]
    b = np.arange(BAND)[None, :]
    corner = _rel_bucket_np(b - a + BAND)
    m = np.arange(META_KEYS)[:, None] - (META_KEYS - N_META)
    dist0 = N_META + j - m
    meta0 = np.where(m >= 0, _rel_bucket_np(np.maximum(dist0, 0)), -1)
    meta_far = np.where(m >= 0, REL_BUCKETS - 1, -1) + 0 * j
    return (corner.astype(np.int32), diag.astype(np.int32),
            np.stack([meta0, meta_far]).astype(np.int32))


def _bias_kernel(tab_ref, corner_id_ref, diag_id_ref, meta_id_ref, corner_ref, diag_ref, meta_ref):
    h = pl.program_id(0)
    far = tab_ref[REL_BUCKETS - 1, h]

    def build(ids):
        out = jnp.where(ids < 0, NEG_INF, 0.0).astype(F32)
        for b in range(REL_BUCKETS - 1):
            out = jnp.where(ids == b, (tab_ref[b, h] - far) * LOG2E, out)
        return out

    corner_ref[0] = build(corner_id_ref[...])
    diag_ref[0] = build(diag_id_ref[...])
    meta_ref[0] = build(meta_id_ref[...])


def _bias_tiles(rel_bias):
    corner_ids, diag_ids, meta_ids = _bucket_tiles()
    return pl.pallas_call(
        _bias_kernel,
        grid=(DIFF_HEADS,),
        in_specs=[
            pl.BlockSpec(memory_space=pltpu.SMEM),
            _resident((BAND, BAND)),
            _resident((TK, TK)),
            _resident((2, META_KEYS, TQ)),
        ],
        out_specs=[
            pl.BlockSpec((1, BAND, BAND), lambda h: (h, 0, 0)),
            pl.BlockSpec((1, TK, TK), lambda h: (h, 0, 0)),
            pl.BlockSpec((1, 2, META_KEYS, TQ), lambda h: (h, 0, 0, 0)),
        ],
        out_shape=[
            jax.ShapeDtypeStruct((DIFF_HEADS, BAND, BAND), F32),
            jax.ShapeDtypeStruct((DIFF_HEADS, TK, TK), F32),
            jax.ShapeDtypeStruct((DIFF_HEADS, 2, META_KEYS, TQ), F32),
        ],
        compiler_params=_params("arbitrary"),
        name="rel_bias_tiles",
    )(rel_bias, jnp.asarray(corner_ids), jnp.asarray(diag_ids), jnp.asarray(meta_ids))


def _attn_kernel(qT_ref, k_ref, vT_ref, corner_ref, diag_ref, meta_ref, lam_ref, sw_ref, o_ref,
                 s_ref, smax_ref, p_ref, alpha_ref, m_ref, acc_ref, *, n_real, lambda_init):
    qi = pl.program_id(2)
    qT = qT_ref[0]
    sub = lax.broadcasted_iota(jnp.int32, (DIFF_VDIM, 1), 0)
    q_maps = (jnp.where(sub < DIFF_HEAD_DIM, qT, jnp.zeros_like(qT)),
              jnp.where(sub >= DIFF_HEAD_DIM, qT, jnp.zeros_like(qT)))
    all_groups = tuple(range(TQ // MXU_N))
    upper_groups = all_groups[len(all_groups) // 2:]

    def lanes(qg):
        return slice(qg * MXU_N, (qg + 1) * MXU_N)

    def v_ext(lo, n):
        return jnp.concatenate([vT_ref[0, :, pl.ds(lo, n)],
                                jnp.ones((V_ROWS - DIFF_VDIM, n), BF16)], axis=0)

    def key_lo(step_idx):
        if isinstance(step_idx, int):
            return max(step_idx, 0) * TK
        return pl.multiple_of(jnp.maximum(step_idx, 0) * TK, TK)

    def scores_into(slot, lo, groups):
        k_blk = k_ref[pl.ds(lo, TK), :]
        for c in range(2):
            for qg in groups:
                s = jnp.dot(k_blk, q_maps[c][:, lanes(qg)], preferred_element_type=F32)
                s_ref[slot, c, :, lanes(qg)] = s
                smax_ref[slot, c, :, lanes(qg)] = jnp.max(s, axis=0, keepdims=True)

    def add_corner(slot, lane0):
        for c in range(2):
            blk = (slot, c, slice(TK - BAND, TK), slice(lane0, lane0 + BAND))
            s_ref[blk] = s_ref[blk] + corner_ref[0]
            qs = slice(lane0, lane0 + MXU_N)
            smax_ref[slot, c, :, qs] = jnp.max(s_ref[slot, c, :, qs], axis=0, keepdims=True)

    def accumulate(slot, lo, groups):
        vx = v_ext(lo, TK)
        for c in range(2):
            for qg in groups:
                qs = lanes(qg)
                acc_ref[c, :, qs] = (alpha_ref[slot, c, :, qs] * acc_ref[c, :, qs]
                                     + jnp.dot(vx, p_ref[slot, c, :, qs], preferred_element_type=F32))

    def step(slot, idx, groups, diag_lane0=None, nxt=True, lag=all_groups):
        if nxt:
            scores_into(1 - slot, key_lo(idx - 1), all_groups)
        for c in range(2):
            for qg in groups:
                qs = lanes(qg)
                s = s_ref[slot, c, :, qs]
                on_diag = diag_lane0 is not None and 0 <= qg * MXU_N - diag_lane0 < TK
                if on_diag:
                    s = s + diag_ref[0, :, qg * MXU_N - diag_lane0:(qg + 1) * MXU_N - diag_lane0]
                    s_max = jnp.max(s, axis=0, keepdims=True)
                else:
                    s_max = smax_ref[slot, c, :, qs]
                m_old = m_ref[c, :, qs]
                m_new = jnp.maximum(m_old, s_max)
                alpha_ref[slot, c, :, qs] = jnp.exp2(m_old - m_new)
                p_ref[slot, c, :, qs] = jnp.exp2(s - m_new).astype(BF16)
                m_ref[c, :, qs] = m_new
        if lag:
            accumulate(1 - slot, key_lo(idx + 1), lag)

    meta_lo = n_real + CHUNK - META_KEYS
    k_m = k_ref[meta_lo:meta_lo + META_KEYS, :]
    vx_m = v_ext(meta_lo, META_KEYS)
    bias_m = meta_ref[0, jnp.minimum(qi, 1)]
    for c in range(2):
        s = jnp.dot(k_m, q_maps[c], preferred_element_type=F32) + bias_m
        m_c = jnp.max(s, axis=0, keepdims=True)
        m_ref[c] = m_c
        acc_ref[c] = jnp.dot(vx_m, jnp.exp2(s - m_c).astype(BF16), preferred_element_type=F32)

    scores_into(0, key_lo(2 * qi + 1), upper_groups)
    step(0, 2 * qi + 1, upper_groups, diag_lane0=TK, lag=())
    add_corner(1, TK)
    step(1, 2 * qi, all_groups, diag_lane0=0, lag=upper_groups)

    @pl.when(qi > 0)
    def _():
        add_corner(0, 0)

        def pair(i, carry):
            idx = 2 * qi - 1 - 2 * i
            step(0, idx, all_groups)
            step(1, idx - 1, all_groups)
            return carry

        lax.fori_loop(0, qi - 1, pair, 0)
        step(0, 1, all_groups)
        step(1, 0, all_groups, nxt=False)

    accumulate(1, key_lo(0), all_groups)

    lv = lam_ref[...]
    lam = (jnp.exp(jnp.sum(lv[0:1] * lv[1:2], axis=-1, keepdims=True))
           - jnp.exp(jnp.sum(lv[2:3] * lv[3:4], axis=-1, keepdims=True)) + lambda_init)
    o = [acc_ref[c, 0:DIFF_VDIM, :] / acc_ref[c, DIFF_VDIM:DIFF_VDIM + 1, :] for c in range(2)]
    o = o[0] - lam * o[1]
    o = o * lax.rsqrt(jnp.mean(o * o, axis=0, keepdims=True) + EPS)
    o = o * sw_ref[...] * (1.0 - lambda_init)
    o_ref[...] = o.T.astype(BF16)


def _attention(qT, k, vT, corner, diag, meta, lamv, sw, *, batch, n_real, lp, lambda_init):
    nq = n_real // TQ
    return pl.pallas_call(
        functools.partial(_attn_kernel, n_real=n_real, lambda_init=lambda_init),
        grid=(batch, DIFF_HEADS, nq),
        in_specs=[
            pl.BlockSpec((1, DIFF_VDIM, TQ), lambda b, h, i: (b, h, i)),
            pl.BlockSpec((lp, DIFF_VDIM), lambda b, h, i: (b, h)),
            pl.BlockSpec((1, DIFF_VDIM, lp), lambda b, h, i: (b, h, 0)),
            pl.BlockSpec((1, BAND, BAND), lambda b, h, i: (h, 0, 0)),
            pl.BlockSpec((1, TK, TK), lambda b, h, i: (h, 0, 0)),
            pl.BlockSpec((1, 2, META_KEYS, TQ), lambda b, h, i: (h, 0, 0, 0)),
            _resident((4, DIFF_HEAD_DIM)),
            _resident((DIFF_VDIM, 1)),
        ],
        out_specs=pl.BlockSpec((TQ, DIFF_VDIM), lambda b, h, i: (b * nq + i, h)),
        out_shape=jax.ShapeDtypeStruct((batch * n_real, D_MODEL), BF16),
        scratch_shapes=[
            pltpu.VMEM((2, 2, TK, TQ), F32),
            pltpu.VMEM((2, 2, 1, TQ), F32),
            pltpu.VMEM((2, 2, TK, TQ), BF16),
            pltpu.VMEM((2, 2, 1, TQ), F32),
            pltpu.VMEM((2, 1, TQ), F32),
            pltpu.VMEM((2, V_ROWS, TQ), F32),
        ],
        compiler_params=_params("parallel", "parallel", "arbitrary"),
        name="diff_attention",
    )(qT, k, vT, corner, diag, meta, lamv, sw)


def _mm_res3_kernel(a_ref, w_ref, r_ref, o_ref):
    o_ref[...] = r_ref[0] + jnp.dot(a_ref[...], w_ref[...], preferred_element_type=F32)


def _mm_res_real(a, w, res3, *, batch, n_real):
    tiles = n_real // ROW_TILE
    k = a.shape[1]
    return pl.pallas_call(
        _mm_res3_kernel,
        grid=(batch, tiles),
        in_specs=[
            pl.BlockSpec((ROW_TILE, k), lambda b, i: (b * tiles + i, 0)),
            _resident((k, D_MODEL)),
            pl.BlockSpec((1, ROW_TILE, D_MODEL), lambda b, i: (b, i, 0)),
        ],
        out_specs=pl.BlockSpec((ROW_TILE, D_MODEL), lambda b, i: (b * tiles + i, 0)),
        out_shape=jax.ShapeDtypeStruct((batch * n_real, D_MODEL), F32),
        compiler_params=_params("parallel", "parallel"),
        name="attn_out_residual",
    )(a, w, res3)


def kernel(x, meta_tokens, norm_w, ssm_in_w, ssm_conv_w, ssm_conv_b, ssm_dt_bias, ssm_a_log, ssm_d,
           ssm_norm_w, ssm_out_w, kv_norm_w, w_kv, w_q, lam_q1, lam_k1, lam_q2, lam_k2, subln_w,
           w_o, rel_bias, ffn_w_gu, ffn_w_down, final_norm_w):
    batch, n_real, d = x.shape
    assert d == D_MODEL and n_real % TQ == 0 and n_real % ROW_TILE == 0 and TQ == 2 * TK
    assert norm_w.shape[0] == 2 and ssm_in_w.shape[0] == 1 and w_q.shape[0] == 1
    lp = n_real + CHUNK
    n_pad = CHUNK - N_META
    n_chunks = lp // CHUNK
    rows = batch * lp
    assert rows % ROW_TILE == 0
    row2 = lambda v: v.reshape(1, -1).astype(F32)

    tail = jnp.concatenate([jnp.zeros((n_pad, D_MODEL), F32), meta_tokens.astype(F32)], axis=0)
    h0 = jnp.concatenate([x, jnp.broadcast_to(tail[None], (batch, CHUNK, D_MODEL))], axis=1)
    h0 = h0.reshape(rows, D_MODEL)

    in_w = ssm_in_w[0]
    w_zx = in_w[:, :D_INNER + CONV_DIM].astype(BF16)
    w_dt = in_w[:, D_INNER + CONV_DIM:]
    z, xc, dt, dtT = _in_proj(h0, row2(norm_w[0, 0]), w_zx, w_dt, w_dt.T, ssm_conv_w[0].astype(F32),
                              row2(ssm_conv_b[0]), batch=batch, n_chunks=n_chunks)
    y = _ssd(z, xc, dt, dtT, row2(ssm_dt_bias[0]), ssm_dt_bias[0].reshape(-1, 1),
             row2(ssm_a_log[0]), ssm_a_log[0].reshape(-1, 1),
             row2(jnp.repeat(ssm_d[0], SSM_HEADDIM)), row2(ssm_norm_w[0]),
             batch=batch, n_chunks=n_chunks, n_pad=n_pad)
    h1 = _mm_res(y, ssm_out_w[0].astype(BF16), h0, rows=rows)
    gu0 = ffn_w_gu[0].astype(BF16)
    h2 = _ffn(h1, row2(norm_w[0, 1]), gu0[:, :FFN_HIDDEN], gu0[:, FFN_HIDDEN:],
              ffn_w_down[0].astype(BF16), row2(final_norm_w), final_norm=False)

    layer = 1
    lambda_init = 0.8 - 0.6 * math.exp(-0.3 * layer)
    k, vT, qT = _kvq(h2, row2(kv_norm_w), row2(norm_w[1, 0]), w_kv[:, :D_MODEL].astype(BF16),
                     w_kv[:, D_MODEL:].T.astype(BF16), w_q[0].T.astype(BF16), batch=batch, lp=lp)
    corner, diag, meta = _bias_tiles(rel_bias.astype(F32))
    lamv = jnp.stack([lam_q1[0], lam_k1[0], lam_q2[0], lam_k2[0]]).astype(F32)
    attn = _attention(qT, k, vT, corner, diag, meta, lamv, subln_w[0].reshape(-1, 1).astype(F32),
                      batch=batch, n_real=n_real, lp=lp, lambda_init=lambda_init)
    h3 = _mm_res_real(attn, w_o[0].astype(BF16), h2.reshape(batch, lp, D_MODEL),
                      batch=batch, n_real=n_real)
    gu1 = ffn_w_gu[1].astype(BF16)
    out = _ffn(h3, row2(norm_w[1, 1]), gu1[:, :FFN_HIDDEN], gu1[:, FFN_HIDDEN:],
               ffn_w_down[1].astype(BF16), row2(final_norm_w), final_norm=True)
    return out.reshape(batch, n_real, D_MODEL)
```

```python
import functools
import math

import numpy as np
import jax
import jax.numpy as jnp
from jax import lax
from jax.experimental import pallas as pl
from jax.experimental.pallas import tpu as pltpu

F32 = jnp.float32
BF16 = jnp.bfloat16
HIGHEST = lax.Precision.HIGHEST

D_MODEL = 1024
N_META = 16
EPS = 1e-6
NEG_INF = -1e30
SSM_HEADDIM = 64
SSM_HEADS = 32
SSM_GROUPS = 8
SSM_HPG = SSM_HEADS // SSM_GROUPS
SSM_STATE = 128
SSM_CONV = 4
D_INNER = SSM_HEADS * SSM_HEADDIM
GROUP_W = D_INNER // SSM_GROUPS
CONV_DIM = D_INNER + 2 * SSM_GROUPS * SSM_STATE
DIFF_HEADS = 8
DIFF_HEAD_DIM = 64
DIFF_VDIM = 2 * DIFF_HEAD_DIM
REL_BUCKETS = 32
REL_MAX_DIST = 128
FFN_HIDDEN = 2816

SUBLANES = 8
LANES = 128
CHUNK = 256
ROW_TILE = 512
CONV_TILE = 512
MXU_N = 256
TQ = 1024
TK = 512
BAND = REL_MAX_DIST
META_KEYS = 128
V_ROWS = DIFF_VDIM + 16
LOG2E = math.log2(math.e)
FFN_TILE = 1408
VMEM_LIMIT = 56 * 1024 * 1024


def _resident(shape):
    nd = len(shape)
    return pl.BlockSpec(shape, lambda *_: (0,) * nd, pipeline_mode=pl.Buffered(1))


def _params(*sem):
    return pltpu.CompilerParams(dimension_semantics=sem, vmem_limit_bytes=VMEM_LIMIT)


def _rmsnorm(x, w):
    return x * lax.rsqrt(jnp.mean(x * x, axis=-1, keepdims=True) + EPS) * w


def _softplus(v):
    return jnp.maximum(v, 0.0) + jnp.log(1.0 + jnp.exp(-jnp.abs(v)))


def _silu(v):
    return v * jax.nn.sigmoid(v)


def _split3(v):
    hi = v.astype(BF16)
    r1 = v - hi.astype(F32)
    mid = r1.astype(BF16)
    lo = (r1 - mid.astype(F32)).astype(BF16)
    return hi, mid, lo


_NT = (((1,), (1,)), ((), ()))


def _in_proj_kernel(x_ref, nw_ref, w_ref, wdh_ref, wdl_ref, cw_ref, cb_ref,
                    z_ref, xc_ref, dt_ref, dtT_ref, xe_ref):
    Q = CHUNK
    c = pl.program_id(1)

    @pl.when(c == 0)
    def _():
        xe_ref[0:SUBLANES, :] = jnp.zeros((SUBLANES, CONV_DIM), F32)

    xn = _rmsnorm(x_ref[...], nw_ref[...])
    xb = xn.astype(BF16)
    z_tile = CONV_TILE * D_INNER // CONV_DIM
    for t in range(CONV_DIM // CONV_TILE):
        c0 = t * CONV_TILE
        cs = slice(c0, c0 + CONV_TILE)
        pre = jnp.dot(xb, w_ref[:, D_INNER + c0:D_INNER + c0 + CONV_TILE], preferred_element_type=F32)
        xe_ref[SUBLANES:SUBLANES + Q, cs] = pre
        acc = cb_ref[:, cs] + cw_ref[SSM_CONV - 1:SSM_CONV, cs] * pre
        for k in range(SSM_CONV - 1):
            off = SUBLANES - (SSM_CONV - 1) + k
            acc = acc + cw_ref[k:k + 1, cs] * xe_ref[off:off + Q, cs]
        xc_ref[:, cs] = _silu(acc).astype(BF16)
        zs = slice(t * z_tile, (t + 1) * z_tile)
        z_ref[:, zs] = jnp.dot(xb, w_ref[:, zs], preferred_element_type=F32).astype(BF16)
    xe_ref[0:SUBLANES, :] = xe_ref[Q:Q + SUBLANES, :]
    x_lo = (xn - xb.astype(F32)).astype(BF16)
    dt = (jnp.dot(xb, wdh_ref[...], preferred_element_type=F32)
          + jnp.dot(x_lo, wdh_ref[...], preferred_element_type=F32)
          + jnp.dot(xb, wdl_ref[...], preferred_element_type=F32))
    dt_ref[...] = dt[:, 0:SSM_HEADS]
    dtT_ref[...] = dt.T[0:SSM_HEADS, :]


def _scan_block(n_chunks):
    return lambda b, c: b * n_chunks + (c + n_chunks - 1) % n_chunks


def _in_proj(h2d, nw, w_zx, w_dt_hi, w_dt_lo, cw, cb, *, batch, n_chunks):
    rows = h2d.shape[0]
    blk = _scan_block(n_chunks)
    return pl.pallas_call(
        _in_proj_kernel,
        grid=(batch, n_chunks),
        in_specs=[
            pl.BlockSpec((CHUNK, D_MODEL), lambda b, c: (blk(b, c), 0)),
            _resident((1, D_MODEL)),
            _resident((D_MODEL, D_INNER + CONV_DIM)),
            _resident((D_MODEL, LANES)),
            _resident((D_MODEL, LANES)),
            _resident((SSM_CONV, CONV_DIM)),
            _resident((1, CONV_DIM)),
        ],
        out_specs=[
            pl.BlockSpec((CHUNK, D_INNER), lambda b, c: (blk(b, c), 0)),
            pl.BlockSpec((CHUNK, CONV_DIM), lambda b, c: (blk(b, c), 0)),
            pl.BlockSpec((CHUNK, SSM_HEADS), lambda b, c: (blk(b, c), 0)),
            pl.BlockSpec((SSM_HEADS, CHUNK), lambda b, c: (0, blk(b, c))),
        ],
        out_shape=[
            jax.ShapeDtypeStruct((rows, D_INNER), BF16),
            jax.ShapeDtypeStruct((rows, CONV_DIM), BF16),
            jax.ShapeDtypeStruct((rows, SSM_HEADS), F32),
            jax.ShapeDtypeStruct((SSM_HEADS, rows), F32),
        ],
        scratch_shapes=[pltpu.VMEM((CHUNK + 2 * SUBLANES, CONV_DIM), F32)],
        compiler_params=_params("arbitrary", "arbitrary"),
        name="in_proj",
    )(h2d, nw, w_zx, w_dt_hi, w_dt_lo, cw, cb)


def _ssd_kernel(xc_ref, dt_ref, dtT_ref, dtb_ref, dtbT_ref, alog_ref, alogT_ref,
                dsk_ref, y_ref, state_ref, *, n_pad):
    Q = CHUNK
    c = pl.program_id(1)

    @pl.when(c == 0)
    def _():
        state_ref[...] = jnp.zeros_like(state_ref)

    first_valid = jnp.where(c == 0, n_pad, 0)
    row = lax.broadcasted_iota(jnp.int32, (Q, 1), 0)
    col = lax.broadcasted_iota(jnp.int32, (1, Q), 1)
    dtc = jnp.where(row >= first_valid, _softplus(dt_ref[...] + dtb_ref[...]), 0.0)
    dtr = jnp.where(col >= first_valid, _softplus(dtT_ref[...] + dtbT_ref[...]), 0.0)
    dac = dtc * (-jnp.exp(alog_ref[...]))
    dar = dtr * (-jnp.exp(alogT_ref[...]))
    ii = lax.broadcasted_iota(jnp.int32, (Q, Q), 0)
    jj = lax.broadcasted_iota(jnp.int32, (Q, Q), 1)
    tri = ii >= jj
    tri_b = tri.astype(BF16)
    tri_bt = (ii <= jj).astype(BF16)
    ac2 = sum(jnp.dot(tri_b, t, preferred_element_type=F32) for t in _split3(dac)) * LOG2E
    ar2 = sum(jnp.dot(t, tri_bt, preferred_element_type=F32) for t in _split3(dar)) * LOG2E
    ar2_dt = ar2 - jnp.where(dtr > 0.0, jnp.log2(dtr), NEG_INF)
    a_last2 = ar2[:, Q - 1:Q]
    w_state = jnp.exp2(a_last2 - ar2_dt)
    e_last = jnp.exp2(a_last2)
    eacs_c = jnp.exp2(ac2)
    lane = lax.broadcasted_iota(jnp.int32, (1, GROUP_W), 1)

    for g in range(SSM_GROUPS):
        b_g = xc_ref[:, D_INNER + g * SSM_STATE:D_INNER + (g + 1) * SSM_STATE]
        c_g = xc_ref[:, D_INNER + (SSM_GROUPS + g) * SSM_STATE:D_INNER + (SSM_GROUPS + g + 1) * SSM_STATE]
        x_g = xc_ref[:, g * GROUP_W:(g + 1) * GROUP_W]
        cb = lax.dot_general(c_g, b_g, _NT, preferred_element_type=F32)
        b_t = b_g.astype(F32).T
        s_old = state_ref[g]
        y_off = jnp.dot(c_g, s_old.astype(BF16), preferred_element_type=F32)
        y_diag = jnp.zeros((Q, GROUP_W), F32)
        s_add = jnp.zeros((SSM_STATE, GROUP_W), F32)
        scale = jnp.zeros((Q, GROUP_W), F32)
        sdec = jnp.zeros((1, GROUP_W), F32)
        for r in range(SSM_HPG):
            h = g * SSM_HPG + r
            decay_dt = jnp.exp2(jnp.where(tri, ac2[:, h:h + 1] - ar2_dt[h:h + 1, :], NEG_INF))
            w = (cb * decay_dt).astype(BF16)
            in_head = (lane >= r * SSM_HEADDIM) & (lane < (r + 1) * SSM_HEADDIM)
            x_r = jnp.where(in_head, x_g, jnp.zeros_like(x_g))
            y_diag = y_diag + jnp.dot(w, x_r, preferred_element_type=F32)
            s_add = s_add + jnp.dot((b_t * w_state[h:h + 1, :]).astype(BF16), x_r,
                                    preferred_element_type=F32)
            scale = jnp.where(in_head, eacs_c[:, h:h + 1], scale)
            sdec = jnp.where(in_head, e_last[h:h + 1, :], sdec)
        state_ref[g] = s_old * sdec + s_add
        gs = slice(g * GROUP_W, (g + 1) * GROUP_W)
        y_ref[:, gs] = (y_diag + y_off * scale + x_g.astype(F32) * dsk_ref[:, gs]).astype(BF16)


def _ssd(xc, dt, dtT, dtb, dtbT, alog, alogT, dsk, *, batch, n_chunks, n_pad):
    rows = xc.shape[0]
    blk = _scan_block(n_chunks)
    return pl.pallas_call(
        functools.partial(_ssd_kernel, n_pad=n_pad),
        grid=(batch, n_chunks),
        in_specs=[
            pl.BlockSpec((CHUNK, CONV_DIM), lambda b, c: (blk(b, c), 0)),
            pl.BlockSpec((CHUNK, SSM_HEADS), lambda b, c: (blk(b, c), 0)),
            pl.BlockSpec((SSM_HEADS, CHUNK), lambda b, c: (0, blk(b, c))),
            _resident((1, SSM_HEADS)),
            _resident((SSM_HEADS, 1)),
            _resident((1, SSM_HEADS)),
            _resident((SSM_HEADS, 1)),
            _resident((1, D_INNER)),
        ],
        out_specs=pl.BlockSpec((CHUNK, D_INNER), lambda b, c: (blk(b, c), 0)),
        out_shape=jax.ShapeDtypeStruct((rows, D_INNER), BF16),
        scratch_shapes=[pltpu.VMEM((SSM_GROUPS, SSM_STATE, GROUP_W), F32)],
        compiler_params=_params("arbitrary", "arbitrary"),
        name="ssd_scan",
    )(xc, dt, dtT, dtb, dtbT, alog, alogT, dsk)


def _swiglu_residual(x, nw_ref, wg_ref, wv_ref, wd_ref):
    xb = _rmsnorm(x, nw_ref[...]).astype(BF16)
    acc = x
    for t0 in range(0, FFN_HIDDEN, FFN_TILE):
        ts = slice(t0, t0 + FFN_TILE)
        gate = jnp.dot(xb, wg_ref[:, ts], preferred_element_type=F32)
        val = jnp.dot(xb, wv_ref[:, ts], preferred_element_type=F32)
        act = (_silu(gate) * val).astype(BF16)
        acc = acc + jnp.dot(act, wd_ref[ts, :], preferred_element_type=F32)
    return acc


def _ssd_out_ffn_kernel(y_ref, z_ref, gnw_ref, wo_ref, r_ref, nw_ref, wg_ref, wv_ref, wd_ref, o_ref):
    parts = []
    for g in range(SSM_GROUPS):
        gs = slice(g * GROUP_W, (g + 1) * GROUP_W)
        y = y_ref[:, gs].astype(F32) * _silu(z_ref[:, gs].astype(F32))
        parts.append(_rmsnorm(y, gnw_ref[:, gs]).astype(BF16))
    a = jnp.concatenate(parts, axis=-1)
    h = r_ref[...] + jnp.dot(a, wo_ref[...], preferred_element_type=F32)
    o_ref[...] = _swiglu_residual(h, nw_ref, wg_ref, wv_ref, wd_ref)


def _ssd_out_ffn(y, z, gnw, wo, res, nw, wg, wv, wd):
    rows = y.shape[0]
    tile = lambda w: pl.BlockSpec((ROW_TILE, w), lambda i: (i, 0))
    return pl.pallas_call(
        _ssd_out_ffn_kernel,
        grid=(rows // ROW_TILE,),
        in_specs=[
            tile(D_INNER), tile(D_INNER), _resident((1, D_INNER)), _resident((D_INNER, D_MODEL)),
            tile(D_MODEL), _resident((1, D_MODEL)), _resident((D_MODEL, FFN_HIDDEN)),
            _resident((D_MODEL, FFN_HIDDEN)), _resident((FFN_HIDDEN, D_MODEL)),
        ],
        out_specs=tile(D_MODEL),
        out_shape=jax.ShapeDtypeStruct((rows, D_MODEL), F32),
        compiler_params=_params("parallel"),
        name="ssd_out_ffn",
    )(y, z, gnw, wo, res, nw, wg, wv, wd)


def _attn_out_ffn_kernel(a_ref, wo_ref, r_ref, nw_ref, wg_ref, wv_ref, wd_ref, fnw_ref, o_ref):
    h = r_ref[0] + jnp.dot(a_ref[...], wo_ref[...], preferred_element_type=F32)
    o_ref[...] = _rmsnorm(_swiglu_residual(h, nw_ref, wg_ref, wv_ref, wd_ref), fnw_ref[...])


def _attn_out_ffn(a, wo, res3, nw, wg, wv, wd, fnw, *, batch, n_real):
    tiles = n_real // ROW_TILE
    return pl.pallas_call(
        _attn_out_ffn_kernel,
        grid=(batch, tiles),
        in_specs=[
            pl.BlockSpec((ROW_TILE, D_MODEL), lambda b, i: (b * tiles + i, 0)),
            _resident((D_MODEL, D_MODEL)),
            pl.BlockSpec((1, ROW_TILE, D_MODEL), lambda b, i: (b, i, 0)),
            _resident((1, D_MODEL)), _resident((D_MODEL, FFN_HIDDEN)),
            _resident((D_MODEL, FFN_HIDDEN)), _resident((FFN_HIDDEN, D_MODEL)),
            _resident((1, D_MODEL)),
        ],
        out_specs=pl.BlockSpec((ROW_TILE, D_MODEL), lambda b, i: (b * tiles + i, 0)),
        out_shape=jax.ShapeDtypeStruct((batch * n_real, D_MODEL), F32),
        compiler_params=_params("parallel", "parallel"),
        name="attn_out_ffn",
    )(a, wo, res3, nw, wg, wv, wd, fnw)


def _kvq_kernel(h_ref, kvnw_ref, qnw_ref, wk_ref, wvT_ref, wqT_ref, k_ref, vT_ref, qT_ref):
    x = h_ref[...]
    inv = lax.rsqrt(jnp.mean(x * x, axis=-1, keepdims=True) + EPS)
    xkv = (x * inv * kvnw_ref[...]).astype(BF16)
    xq = (x * inv * qnw_ref[...]).astype(BF16)
    k_ref[...] = jnp.dot(xkv, wk_ref[...], preferred_element_type=F32).astype(BF16)
    vT_ref[0] = lax.dot_general(wvT_ref[...], xkv, _NT, preferred_element_type=F32).astype(BF16)
    qT = lax.dot_general(wqT_ref[...], xq, _NT, preferred_element_type=F32)
    qT_ref[0] = (qT * (DIFF_HEAD_DIM ** -0.5 * LOG2E)).astype(BF16)


def _kvq(h, kvnw, qnw, wk, wvT, wqT, *, batch, lp):
    tiles = lp // CHUNK
    return pl.pallas_call(
        _kvq_kernel,
        grid=(batch, tiles),
        in_specs=[
            pl.BlockSpec((CHUNK, D_MODEL), lambda b, i: (b * tiles + i, 0)),
            _resident((1, D_MODEL)),
            _resident((1, D_MODEL)),
            _resident((D_MODEL, D_MODEL)),
            _resident((D_MODEL, D_MODEL)),
            _resident((D_MODEL, D_MODEL)),
        ],
        out_specs=[
            pl.BlockSpec((CHUNK, D_MODEL), lambda b, i: (b * tiles + i, 0)),
            pl.BlockSpec((1, D_MODEL, CHUNK), lambda b, i: (b, 0, i)),
            pl.BlockSpec((1, D_MODEL, CHUNK), lambda b, i: (b, 0, i)),
        ],
        out_shape=[
            jax.ShapeDtypeStruct((batch * lp, D_MODEL), BF16),
            jax.ShapeDtypeStruct((batch, D_MODEL, lp), BF16),
            jax.ShapeDtypeStruct((batch, D_MODEL, lp), BF16),
        ],
        compiler_params=_params("parallel", "parallel"),
        name="kvq_proj",
    )(h, kvnw, qnw, wk, wvT, wqT)


def _rel_bucket_np(n):
    n = np.asarray(n)
    max_exact = REL_BUCKETS // 2
    nf = np.maximum(n, 1).astype(np.float32)
    large = max_exact + (np.log(nf / np.float32(max_exact)) / np.float32(math.log(REL_MAX_DIST / max_exact))
                         * np.float32(REL_BUCKETS - max_exact)).astype(np.int32)
    large = np.minimum(large, REL_BUCKETS - 1)
    return np.where(n < max_exact, n, large).astype(np.int32)


def _bucket_tiles():
    i = np.arange(TK)[:, None]
    jd = np.arange(TK)[None, :]
    j = np.arange(TQ)[None, :]
    diag = np.where(jd - i >= 0, _rel_bucket_np(np.maximum(jd - i, 0)), -1)
    a = np.arange(BAND)[:, None]
    b = np.arange(BAND)[None, :]
    corner = _rel_bucket_np(b - a + BAND)
    m = np.arange(META_KEYS)[:, None] - (META_KEYS - N_META)
    dist0 = N_META + j - m
    meta0 = np.where(m >= 0, _rel_bucket_np(np.maximum(dist0, 0)), -1)
    meta_far = np.where(m >= 0, REL_BUCKETS - 1, -1) + 0 * j
    return (corner.astype(np.int32), diag.astype(np.int32),
            np.stack([meta0, meta_far]).astype(np.int32))


def _bias_kernel(tab_ref, corner_id_ref, diag_id_ref, meta_id_ref, corner_ref, diag_ref, meta_ref):
    h = pl.program_id(0)
    far = tab_ref[REL_BUCKETS - 1, h]

    def build(ids):
        out = jnp.where(ids < 0, NEG_INF, 0.0).astype(F32)
        for b in range(REL_BUCKETS - 1):
            out = jnp.where(ids == b, (tab_ref[b, h] - far) * LOG2E, out)
        return out

    corner_ref[0] = build(corner_id_ref[...])
    diag_ref[0] = build(diag_id_ref[...])
    meta_ref[0] = build(meta_id_ref[...])


def _bias_tiles(rel_bias):
    corner_ids, diag_ids, meta_ids = _bucket_tiles()
    return pl.pallas_call(
        _bias_kernel,
        grid=(DIFF_HEADS,),
        in_specs=[
            pl.BlockSpec(memory_space=pltpu.SMEM),
            _resident((BAND, BAND)),
            _resident((TK, TK)),
            _resident((2, META_KEYS, TQ)),
        ],
        out_specs=[
            pl.BlockSpec((1, BAND, BAND), lambda h: (h, 0, 0)),
            pl.BlockSpec((1, TK, TK), lambda h: (h, 0, 0)),
            pl.BlockSpec((1, 2, META_KEYS, TQ), lambda h: (h, 0, 0, 0)),
        ],
        out_shape=[
            jax.ShapeDtypeStruct((DIFF_HEADS, BAND, BAND), F32),
            jax.ShapeDtypeStruct((DIFF_HEADS, TK, TK), F32),
            jax.ShapeDtypeStruct((DIFF_HEADS, 2, META_KEYS, TQ), F32),
        ],
        compiler_params=_params("arbitrary"),
        name="rel_bias_tiles",
    )(rel_bias, jnp.asarray(corner_ids), jnp.asarray(diag_ids), jnp.asarray(meta_ids))


def _attn_kernel(qT_ref, k_ref, vT_ref, corner_ref, diag_ref, meta_ref, lam_ref, sw_ref, o_ref,
                 s_ref, smax_ref, p_ref, alpha_ref, m_ref, acc_ref, *, n_real, lambda_init):
    qi = pl.program_id(2)
    qT = qT_ref[0]
    sub = lax.broadcasted_iota(jnp.int32, (DIFF_VDIM, 1), 0)
    q_maps = (jnp.where(sub < DIFF_HEAD_DIM, qT, jnp.zeros_like(qT)),
              jnp.where(sub >= DIFF_HEAD_DIM, qT, jnp.zeros_like(qT)))
    all_groups = tuple(range(TQ // MXU_N))
    upper_groups = all_groups[len(all_groups) // 2:]

    def lanes(qg):
        return slice(qg * MXU_N, (qg + 1) * MXU_N)

    def v_ext(lo, n):
        return jnp.concatenate([vT_ref[0, :, pl.ds(lo, n)],
                                jnp.ones((V_ROWS - DIFF_VDIM, n), BF16)], axis=0)

    def key_lo(step_idx):
        if isinstance(step_idx, int):
            return max(step_idx, 0) * TK
        return pl.multiple_of(jnp.maximum(step_idx, 0) * TK, TK)

    def scores_into(slot, lo, groups):
        k_blk = k_ref[pl.ds(lo, TK), :]
        for c in range(2):
            for qg in groups:
                s = jnp.dot(k_blk, q_maps[c][:, lanes(qg)], preferred_element_type=F32)
                s_ref[slot, c, :, lanes(qg)] = s
                smax_ref[slot, c, :, lanes(qg)] = jnp.max(s, axis=0, keepdims=True)

    def add_corner(slot, lane0):
        for c in range(2):
            blk = (slot, c, slice(TK - BAND, TK), slice(lane0, lane0 + BAND))
            s_ref[blk] = s_ref[blk] + corner_ref[0]
            qs = slice(lane0, lane0 + MXU_N)
            smax_ref[slot, c, :, qs] = jnp.max(s_ref[slot, c, :, qs], axis=0, keepdims=True)

    def accumulate(slot, lo, groups):
        vx = v_ext(lo, TK)
        for c in range(2):
            for qg in groups:
                qs = lanes(qg)
                acc_ref[c, :, qs] = (alpha_ref[slot, c, :, qs] * acc_ref[c, :, qs]
                                     + jnp.dot(vx, p_ref[slot, c, :, qs], preferred_element_type=F32))

    def step(slot, idx, groups, diag_lane0=None, nxt=True, lag=all_groups):
        if nxt:
            scores_into(1 - slot, key_lo(idx - 1), all_groups)
        for c in range(2):
            for qg in groups:
                qs = lanes(qg)
                s = s_ref[slot, c, :, qs]
                on_diag = diag_lane0 is not None and 0 <= qg * MXU_N - diag_lane0 < TK
                if on_diag:
                    s = s + diag_ref[0, :, qg * MXU_N - diag_lane0:(qg + 1) * MXU_N - diag_lane0]
                    s_max = jnp.max(s, axis=0, keepdims=True)
                else:
                    s_max = smax_ref[slot, c, :, qs]
                m_old = m_ref[c, :, qs]
                m_new = jnp.maximum(m_old, s_max)
                alpha_ref[slot, c, :, qs] = jnp.exp2(m_old - m_new)
                p_ref[slot, c, :, qs] = jnp.exp2(s - m_new).astype(BF16)
                m_ref[c, :, qs] = m_new
        if lag:
            accumulate(1 - slot, key_lo(idx + 1), lag)

    meta_lo = n_real + CHUNK - META_KEYS
    k_m = k_ref[meta_lo:meta_lo + META_KEYS, :]
    vx_m = v_ext(meta_lo, META_KEYS)
    bias_m = meta_ref[0, jnp.minimum(qi, 1)]
    for c in range(2):
        s = jnp.dot(k_m, q_maps[c], preferred_element_type=F32) + bias_m
        m_c = jnp.max(s, axis=0, keepdims=True)
        m_ref[c] = m_c
        acc_ref[c] = jnp.dot(vx_m, jnp.exp2(s - m_c).astype(BF16), preferred_element_type=F32)

    scores_into(0, key_lo(2 * qi + 1), upper_groups)
    step(0, 2 * qi + 1, upper_groups, diag_lane0=TK, lag=())
    add_corner(1, TK)
    step(1, 2 * qi, all_groups, diag_lane0=0, lag=upper_groups)

    @pl.when(qi > 0)
    def _():
        add_corner(0, 0)

        def pair(i, carry):
            idx = 2 * qi - 1 - 2 * i
            step(0, idx, all_groups)
            step(1, idx - 1, all_groups)
            return carry

        lax.fori_loop(0, qi - 1, pair, 0)
        step(0, 1, all_groups)
        step(1, 0, all_groups, nxt=False)

    accumulate(1, key_lo(0), all_groups)

    lv = lam_ref[...]
    lam = (jnp.exp(jnp.sum(lv[0:1] * lv[1:2], axis=-1, keepdims=True))
           - jnp.exp(jnp.sum(lv[2:3] * lv[3:4], axis=-1, keepdims=True)) + lambda_init)
    o = [acc_ref[c, 0:DIFF_VDIM, :] / acc_ref[c, DIFF_VDIM:DIFF_VDIM + 1, :] for c in range(2)]
    o = o[0] - lam * o[1]
    o = o * lax.rsqrt(jnp.mean(o * o, axis=0, keepdims=True) + EPS)
    o = o * sw_ref[...] * (1.0 - lambda_init)
    o_ref[...] = o.T.astype(BF16)


def _attention(qT, k, vT, corner, diag, meta, lamv, sw, *, batch, n_real, lp, lambda_init):
    nq = n_real // TQ
    return pl.pallas_call(
        functools.partial(_attn_kernel, n_real=n_real, lambda_init=lambda_init),
        grid=(batch, DIFF_HEADS, nq),
        in_specs=[
            pl.BlockSpec((1, DIFF_VDIM, TQ), lambda b, h, i: (b, h, i)),
            pl.BlockSpec((lp, DIFF_VDIM), lambda b, h, i: (b, h)),
            pl.BlockSpec((1, DIFF_VDIM, lp), lambda b, h, i: (b, h, 0)),
            pl.BlockSpec((1, BAND, BAND), lambda b, h, i: (h, 0, 0)),
            pl.BlockSpec((1, TK, TK), lambda b, h, i: (h, 0, 0)),
            pl.BlockSpec((1, 2, META_KEYS, TQ), lambda b, h, i: (h, 0, 0, 0)),
            _resident((4, DIFF_HEAD_DIM)),
            _resident((DIFF_VDIM, 1)),
        ],
        out_specs=pl.BlockSpec((TQ, DIFF_VDIM), lambda b, h, i: (b * nq + i, h)),
        out_shape=jax.ShapeDtypeStruct((batch * n_real, D_MODEL), BF16),
        scratch_shapes=[
            pltpu.VMEM((2, 2, TK, TQ), F32),
            pltpu.VMEM((2, 2, 1, TQ), F32),
            pltpu.VMEM((2, 2, TK, TQ), BF16),
            pltpu.VMEM((2, 2, 1, TQ), F32),
            pltpu.VMEM((2, 1, TQ), F32),
            pltpu.VMEM((2, V_ROWS, TQ), F32),
        ],
        compiler_params=_params("parallel", "parallel", "arbitrary"),
        name="diff_attention",
    )(qT, k, vT, corner, diag, meta, lamv, sw)


def kernel(x, meta_tokens, norm_w, ssm_in_w, ssm_conv_w, ssm_conv_b, ssm_dt_bias, ssm_a_log, ssm_d,
           ssm_norm_w, ssm_out_w, kv_norm_w, w_kv, w_q, lam_q1, lam_k1, lam_q2, lam_k2, subln_w,
           w_o, rel_bias, ffn_w_gu, ffn_w_down, final_norm_w):
    batch, n_real, d = x.shape
    assert d == D_MODEL and n_real % TQ == 0 and n_real % ROW_TILE == 0 and TQ == 2 * TK
    assert norm_w.shape[0] == 2 and ssm_in_w.shape[0] == 1 and w_q.shape[0] == 1
    lp = n_real + CHUNK
    n_pad = CHUNK - N_META
    n_chunks = lp // CHUNK
    rows = batch * lp
    assert rows % ROW_TILE == 0
    row2 = lambda v: v.reshape(1, -1).astype(F32)

    tail = jnp.concatenate([jnp.zeros((n_pad, D_MODEL), F32), meta_tokens.astype(F32)], axis=0)
    h0 = jnp.concatenate([x, jnp.broadcast_to(tail[None], (batch, CHUNK, D_MODEL))], axis=1)
    h0 = h0.reshape(rows, D_MODEL)

    in_w = ssm_in_w[0]
    w_zx = in_w[:, :D_INNER + CONV_DIM].astype(BF16)
    w_dt = jnp.pad(in_w[:, D_INNER + CONV_DIM:].astype(F32), ((0, 0), (0, LANES - SSM_HEADS)))
    w_dt_hi = w_dt.astype(BF16)
    w_dt_lo = (w_dt - w_dt_hi.astype(F32)).astype(BF16)
    z, xc, dt, dtT = _in_proj(h0, row2(norm_w[0, 0]), w_zx, w_dt_hi, w_dt_lo, ssm_conv_w[0].astype(F32),
                              row2(ssm_conv_b[0]), batch=batch, n_chunks=n_chunks)
    y = _ssd(xc, dt, dtT, row2(ssm_dt_bias[0]), ssm_dt_bias[0].reshape(-1, 1),
             row2(ssm_a_log[0]), ssm_a_log[0].reshape(-1, 1),
             row2(jnp.repeat(ssm_d[0], SSM_HEADDIM)), batch=batch, n_chunks=n_chunks, n_pad=n_pad)
    gu0 = ffn_w_gu[0].astype(BF16)
    h2 = _ssd_out_ffn(y, z, row2(ssm_norm_w[0]), ssm_out_w[0].astype(BF16), h0, row2(norm_w[0, 1]),
                      gu0[:, :FFN_HIDDEN], gu0[:, FFN_HIDDEN:], ffn_w_down[0].astype(BF16))

    layer = 1
    lambda_init = 0.8 - 0.6 * math.exp(-0.3 * layer)
    k, vT, qT = _kvq(h2, row2(kv_norm_w), row2(norm_w[1, 0]), w_kv[:, :D_MODEL].astype(BF16),
                     w_kv[:, D_MODEL:].T.astype(BF16), w_q[0].T.astype(BF16), batch=batch, lp=lp)
    corner, diag, meta = _bias_tiles(rel_bias.astype(F32))
    lamv = jnp.stack([lam_q1[0], lam_k1[0], lam_q2[0], lam_k2[0]]).astype(F32)
    attn = _attention(qT, k, vT, corner, diag, meta, lamv, subln_w[0].reshape(-1, 1).astype(F32),
                      batch=batch, n_real=n_real, lp=lp, lambda_init=lambda_init)
    gu1 = ffn_w_gu[1].astype(BF16)
    out = _attn_out_ffn(attn, w_o[0].astype(BF16), h2.reshape(batch, lp, D_MODEL), row2(norm_w[1, 1]),
                        gu1[:, :FFN_HIDDEN], gu1[:, FFN_HIDDEN:], ffn_w_down[1].astype(BF16),
                        row2(final_norm_w), batch=batch, n_real=n_real)
    return out.reshape(batch, n_real, D_MODEL)
```

```python
import functools
import math

import numpy as np
import jax
import jax.numpy as jnp
from jax import lax
from jax.experimental import pallas as pl
from jax.experimental.pallas import tpu as pltpu

F32 = jnp.float32
BF16 = jnp.bfloat16

D_MODEL = 1024
N_META = 16
EPS = 1e-6
NEG_INF = -1e30
SSM_HEADDIM = 64
SSM_HEADS = 32
SSM_GROUPS = 8
SSM_HPG = SSM_HEADS // SSM_GROUPS
SSM_STATE = 128
SSM_CONV = 4
D_INNER = SSM_HEADS * SSM_HEADDIM
GROUP_W = D_INNER // SSM_GROUPS
CONV_DIM = D_INNER + 2 * SSM_GROUPS * SSM_STATE
DIFF_HEADS = 8
DIFF_HEAD_DIM = 64
DIFF_VDIM = 2 * DIFF_HEAD_DIM
REL_BUCKETS = 32
REL_MAX_DIST = 128
FFN_HIDDEN = 2816

SUBLANES = 8
LANES = 128
CHUNK = 256
ROW_TILE = 512
CONV_TILE = 512
MXU_N = 256
TK = 512
KEY_STEPS = 4
TQ = KEY_STEPS * TK
BAND = REL_MAX_DIST
META_KEYS = 128
V_ROWS = DIFF_VDIM + 16
LOG2E = math.log2(math.e)
FFN_TILE = 1408
VMEM_LIMIT = 56 * 1024 * 1024


def _resident(shape):
    nd = len(shape)
    return pl.BlockSpec(shape, lambda *_: (0,) * nd, pipeline_mode=pl.Buffered(1))


def _params(*sem):
    return pltpu.CompilerParams(dimension_semantics=sem, vmem_limit_bytes=VMEM_LIMIT)


def _rmsnorm(x, w):
    return x * lax.rsqrt(jnp.mean(x * x, axis=-1, keepdims=True) + EPS) * w


def _softplus(v):
    return jnp.maximum(v, 0.0) + jnp.log(1.0 + jnp.exp(-jnp.abs(v)))


def _silu(v):
    return v * jax.nn.sigmoid(v)


def _split3(v):
    hi = v.astype(BF16)
    r1 = v - hi.astype(F32)
    mid = r1.astype(BF16)
    lo = (r1 - mid.astype(F32)).astype(BF16)
    return hi, mid, lo


_NT = (((1,), (1,)), ((), ()))


def _in_proj_kernel(x_ref, nw_ref, w_ref, wdh_ref, wdl_ref, cw_ref, cb_ref,
                    z_ref, xc_ref, dt_ref, dtT_ref, xe_ref):
    Q = CHUNK
    c = pl.program_id(1)

    @pl.when(c == 0)
    def _():
        xe_ref[0:SUBLANES, :] = jnp.zeros((SUBLANES, CONV_DIM), F32)

    xn = _rmsnorm(x_ref[...], nw_ref[...])
    xb = xn.astype(BF16)
    z_tile = CONV_TILE * D_INNER // CONV_DIM
    for t in range(CONV_DIM // CONV_TILE):
        c0 = t * CONV_TILE
        cs = slice(c0, c0 + CONV_TILE)
        pre = jnp.dot(xb, w_ref[:, D_INNER + c0:D_INNER + c0 + CONV_TILE], preferred_element_type=F32)
        xe_ref[SUBLANES:SUBLANES + Q, cs] = pre
        acc = cb_ref[:, cs] + cw_ref[SSM_CONV - 1:SSM_CONV, cs] * pre
        for k in range(SSM_CONV - 1):
            off = SUBLANES - (SSM_CONV - 1) + k
            acc = acc + cw_ref[k:k + 1, cs] * xe_ref[off:off + Q, cs]
        xc_ref[:, cs] = _silu(acc).astype(BF16)
        zs = slice(t * z_tile, (t + 1) * z_tile)
        z_ref[:, zs] = jnp.dot(xb, w_ref[:, zs], preferred_element_type=F32).astype(BF16)
    xe_ref[0:SUBLANES, :] = xe_ref[Q:Q + SUBLANES, :]
    x_lo = (xn - xb.astype(F32)).astype(BF16)
    dt = (jnp.dot(xb, wdh_ref[...], preferred_element_type=F32)
          + jnp.dot(x_lo, wdh_ref[...], preferred_element_type=F32)
          + jnp.dot(xb, wdl_ref[...], preferred_element_type=F32))
    dt_ref[...] = dt[:, 0:SSM_HEADS]
    dtT_ref[...] = dt.T[0:SSM_HEADS, :]


def _scan_block(n_chunks):
    return lambda b, c: b * n_chunks + (c + n_chunks - 1) % n_chunks


def _in_proj(h2d, nw, w_zx, w_dt_hi, w_dt_lo, cw, cb, *, batch, n_chunks):
    rows = h2d.shape[0]
    blk = _scan_block(n_chunks)
    return pl.pallas_call(
        _in_proj_kernel,
        grid=(batch, n_chunks),
        in_specs=[
            pl.BlockSpec((CHUNK, D_MODEL), lambda b, c: (blk(b, c), 0)),
            _resident((1, D_MODEL)),
            _resident((D_MODEL, D_INNER + CONV_DIM)),
            _resident((D_MODEL, LANES)),
            _resident((D_MODEL, LANES)),
            _resident((SSM_CONV, CONV_DIM)),
            _resident((1, CONV_DIM)),
        ],
        out_specs=[
            pl.BlockSpec((CHUNK, D_INNER), lambda b, c: (blk(b, c), 0)),
            pl.BlockSpec((CHUNK, CONV_DIM), lambda b, c: (blk(b, c), 0)),
            pl.BlockSpec((CHUNK, SSM_HEADS), lambda b, c: (blk(b, c), 0)),
            pl.BlockSpec((SSM_HEADS, CHUNK), lambda b, c: (0, blk(b, c))),
        ],
        out_shape=[
            jax.ShapeDtypeStruct((rows, D_INNER), BF16),
            jax.ShapeDtypeStruct((rows, CONV_DIM), BF16),
            jax.ShapeDtypeStruct((rows, SSM_HEADS), F32),
            jax.ShapeDtypeStruct((SSM_HEADS, rows), F32),
        ],
        scratch_shapes=[pltpu.VMEM((CHUNK + 2 * SUBLANES, CONV_DIM), F32)],
        compiler_params=_params("arbitrary", "arbitrary"),
        name="in_proj",
    )(h2d, nw, w_zx, w_dt_hi, w_dt_lo, cw, cb)


def _ssd_kernel(xc_ref, dt_ref, dtT_ref, dtb_ref, dtbT_ref, alog_ref, alogT_ref,
                dsk_ref, y_ref, state_ref, *, n_pad):
    Q = CHUNK
    c = pl.program_id(1)

    @pl.when(c == 0)
    def _():
        state_ref[...] = jnp.zeros_like(state_ref)

    first_valid = jnp.where(c == 0, n_pad, 0)
    row = lax.broadcasted_iota(jnp.int32, (Q, 1), 0)
    col = lax.broadcasted_iota(jnp.int32, (1, Q), 1)
    dtc = jnp.where(row >= first_valid, _softplus(dt_ref[...] + dtb_ref[...]), 0.0)
    dtr = jnp.where(col >= first_valid, _softplus(dtT_ref[...] + dtbT_ref[...]), 0.0)
    dac = dtc * (-jnp.exp(alog_ref[...]))
    dar = dtr * (-jnp.exp(alogT_ref[...]))
    ii = lax.broadcasted_iota(jnp.int32, (Q, Q), 0)
    jj = lax.broadcasted_iota(jnp.int32, (Q, Q), 1)
    tri = ii >= jj
    tri_b = tri.astype(BF16)
    tri_bt = (ii <= jj).astype(BF16)
    ac2 = sum(jnp.dot(tri_b, t, preferred_element_type=F32) for t in _split3(dac)) * LOG2E
    ar2 = sum(jnp.dot(t, tri_bt, preferred_element_type=F32) for t in _split3(dar)) * LOG2E
    ar2_dt = ar2 - jnp.where(dtr > 0.0, jnp.log2(dtr), NEG_INF)
    a_last2 = ar2[:, Q - 1:Q]
    w_state = jnp.exp2(a_last2 - ar2_dt)
    e_last = jnp.exp2(a_last2)
    eacs_c = jnp.exp2(ac2)
    lane = lax.broadcasted_iota(jnp.int32, (1, GROUP_W), 1)

    for g in range(SSM_GROUPS):
        b_g = xc_ref[:, D_INNER + g * SSM_STATE:D_INNER + (g + 1) * SSM_STATE]
        c_g = xc_ref[:, D_INNER + (SSM_GROUPS + g) * SSM_STATE:D_INNER + (SSM_GROUPS + g + 1) * SSM_STATE]
        x_g = xc_ref[:, g * GROUP_W:(g + 1) * GROUP_W]
        cb = lax.dot_general(c_g, b_g, _NT, preferred_element_type=F32)
        b_t = b_g.astype(F32).T
        s_old = state_ref[g]
        y_off = jnp.dot(c_g, s_old.astype(BF16), preferred_element_type=F32)
        y_diag = jnp.zeros((Q, GROUP_W), F32)
        s_add = jnp.zeros((SSM_STATE, GROUP_W), F32)
        scale = jnp.zeros((Q, GROUP_W), F32)
        sdec = jnp.zeros((1, GROUP_W), F32)
        for r in range(SSM_HPG):
            h = g * SSM_HPG + r
            decay_dt = jnp.exp2(jnp.where(tri, ac2[:, h:h + 1] - ar2_dt[h:h + 1, :], NEG_INF))
            w = (cb * decay_dt).astype(BF16)
            in_head = (lane >= r * SSM_HEADDIM) & (lane < (r + 1) * SSM_HEADDIM)
            x_r = jnp.where(in_head, x_g, jnp.zeros_like(x_g))
            y_diag = y_diag + jnp.dot(w, x_r, preferred_element_type=F32)
            s_add = s_add + jnp.dot((b_t * w_state[h:h + 1, :]).astype(BF16), x_r,
                                    preferred_element_type=F32)
            scale = jnp.where(in_head, eacs_c[:, h:h + 1], scale)
            sdec = jnp.where(in_head, e_last[h:h + 1, :], sdec)
        state_ref[g] = s_old * sdec + s_add
        gs = slice(g * GROUP_W, (g + 1) * GROUP_W)
        y_ref[:, gs] = (y_diag + y_off * scale + x_g.astype(F32) * dsk_ref[:, gs]).astype(BF16)


def _ssd(xc, dt, dtT, dtb, dtbT, alog, alogT, dsk, *, batch, n_chunks, n_pad):
    rows = xc.shape[0]
    blk = _scan_block(n_chunks)
    return pl.pallas_call(
        functools.partial(_ssd_kernel, n_pad=n_pad),
        grid=(batch, n_chunks),
        in_specs=[
            pl.BlockSpec((CHUNK, CONV_DIM), lambda b, c: (blk(b, c), 0)),
            pl.BlockSpec((CHUNK, SSM_HEADS), lambda b, c: (blk(b, c), 0)),
            pl.BlockSpec((SSM_HEADS, CHUNK), lambda b, c: (0, blk(b, c))),
            _resident((1, SSM_HEADS)),
            _resident((SSM_HEADS, 1)),
            _resident((1, SSM_HEADS)),
            _resident((SSM_HEADS, 1)),
            _resident((1, D_INNER)),
        ],
        out_specs=pl.BlockSpec((CHUNK, D_INNER), lambda b, c: (blk(b, c), 0)),
        out_shape=jax.ShapeDtypeStruct((rows, D_INNER), BF16),
        scratch_shapes=[pltpu.VMEM((SSM_GROUPS, SSM_STATE, GROUP_W), F32)],
        compiler_params=_params("arbitrary", "arbitrary"),
        name="ssd_scan",
    )(xc, dt, dtT, dtb, dtbT, alog, alogT, dsk)


def _swiglu_residual(x, nw_ref, wg_ref, wv_ref, wd_ref):
    xb = _rmsnorm(x, nw_ref[...]).astype(BF16)
    acc = x
    for t0 in range(0, FFN_HIDDEN, FFN_TILE):
        ts = slice(t0, t0 + FFN_TILE)
        gate = jnp.dot(xb, wg_ref[:, ts], preferred_element_type=F32)
        val = jnp.dot(xb, wv_ref[:, ts], preferred_element_type=F32)
        act = (_silu(gate) * val).astype(BF16)
        acc = acc + jnp.dot(act, wd_ref[ts, :], preferred_element_type=F32)
    return acc


def _ssd_out_ffn_kernel(y_ref, z_ref, gnw_ref, wo_ref, r_ref, nw_ref, wg_ref, wv_ref, wd_ref, o_ref):
    parts = []
    for g in range(SSM_GROUPS):
        gs = slice(g * GROUP_W, (g + 1) * GROUP_W)
        y = y_ref[:, gs].astype(F32) * _silu(z_ref[:, gs].astype(F32))
        parts.append(_rmsnorm(y, gnw_ref[:, gs]).astype(BF16))
    a = jnp.concatenate(parts, axis=-1)
    h = r_ref[...] + jnp.dot(a, wo_ref[...], preferred_element_type=F32)
    o_ref[...] = _swiglu_residual(h, nw_ref, wg_ref, wv_ref, wd_ref)


def _ssd_out_ffn(y, z, gnw, wo, res, nw, wg, wv, wd):
    rows = y.shape[0]
    tile = lambda w: pl.BlockSpec((ROW_TILE, w), lambda i: (i, 0))
    return pl.pallas_call(
        _ssd_out_ffn_kernel,
        grid=(rows // ROW_TILE,),
        in_specs=[
            tile(D_INNER), tile(D_INNER), _resident((1, D_INNER)), _resident((D_INNER, D_MODEL)),
            tile(D_MODEL), _resident((1, D_MODEL)), _resident((D_MODEL, FFN_HIDDEN)),
            _resident((D_MODEL, FFN_HIDDEN)), _resident((FFN_HIDDEN, D_MODEL)),
        ],
        out_specs=tile(D_MODEL),
        out_shape=jax.ShapeDtypeStruct((rows, D_MODEL), F32),
        compiler_params=_params("parallel"),
        name="ssd_out_ffn",
    )(y, z, gnw, wo, res, nw, wg, wv, wd)


def _attn_out_ffn_kernel(a_ref, wo_ref, r_ref, nw_ref, wg_ref, wv_ref, wd_ref, fnw_ref, o_ref):
    h = r_ref[0] + jnp.dot(a_ref[...], wo_ref[...], preferred_element_type=F32)
    o_ref[...] = _rmsnorm(_swiglu_residual(h, nw_ref, wg_ref, wv_ref, wd_ref), fnw_ref[...])


def _attn_out_ffn(a, wo, res3, nw, wg, wv, wd, fnw, *, batch, n_real):
    tiles = n_real // ROW_TILE
    return pl.pallas_call(
        _attn_out_ffn_kernel,
        grid=(batch, tiles),
        in_specs=[
            pl.BlockSpec((ROW_TILE, D_MODEL), lambda b, i: (b * tiles + i, 0)),
            _resident((D_MODEL, D_MODEL)),
            pl.BlockSpec((1, ROW_TILE, D_MODEL), lambda b, i: (b, i, 0)),
            _resident((1, D_MODEL)), _resident((D_MODEL, FFN_HIDDEN)),
            _resident((D_MODEL, FFN_HIDDEN)), _resident((FFN_HIDDEN, D_MODEL)),
            _resident((1, D_MODEL)),
        ],
        out_specs=pl.BlockSpec((ROW_TILE, D_MODEL), lambda b, i: (b * tiles + i, 0)),
        out_shape=jax.ShapeDtypeStruct((batch * n_real, D_MODEL), F32),
        compiler_params=_params("parallel", "parallel"),
        name="attn_out_ffn",
    )(a, wo, res3, nw, wg, wv, wd, fnw)


def _kvq_kernel(h_ref, kvnw_ref, qnw_ref, wk_ref, wvT_ref, wqT_ref, k_ref, vT_ref, qT_ref):
    x = h_ref[...]
    inv = lax.rsqrt(jnp.mean(x * x, axis=-1, keepdims=True) + EPS)
    xkv = (x * inv * kvnw_ref[...]).astype(BF16)
    xq = (x * inv * qnw_ref[...]).astype(BF16)
    k_ref[...] = jnp.dot(xkv, wk_ref[...], preferred_element_type=F32).astype(BF16)
    vT_ref[0] = lax.dot_general(wvT_ref[...], xkv, _NT, preferred_element_type=F32).astype(BF16)
    qT = lax.dot_general(wqT_ref[...], xq, _NT, preferred_element_type=F32)
    qT_ref[0] = (qT * (DIFF_HEAD_DIM ** -0.5 * LOG2E)).astype(BF16)


def _kvq(h, kvnw, qnw, wk, wvT, wqT, *, batch, lp):
    tiles = lp // CHUNK
    return pl.pallas_call(
        _kvq_kernel,
        grid=(batch, tiles),
        in_specs=[
            pl.BlockSpec((CHUNK, D_MODEL), lambda b, i: (b * tiles + i, 0)),
            _resident((1, D_MODEL)),
            _resident((1, D_MODEL)),
            _resident((D_MODEL, D_MODEL)),
            _resident((D_MODEL, D_MODEL)),
            _resident((D_MODEL, D_MODEL)),
        ],
        out_specs=[
            pl.BlockSpec((CHUNK, D_MODEL), lambda b, i: (b * tiles + i, 0)),
            pl.BlockSpec((1, D_MODEL, CHUNK), lambda b, i: (b, 0, i)),
            pl.BlockSpec((1, D_MODEL, CHUNK), lambda b, i: (b, 0, i)),
        ],
        out_shape=[
            jax.ShapeDtypeStruct((batch * lp, D_MODEL), BF16),
            jax.ShapeDtypeStruct((batch, D_MODEL, lp), BF16),
            jax.ShapeDtypeStruct((batch, D_MODEL, lp), BF16),
        ],
        compiler_params=_params("parallel", "parallel"),
        name="kvq_proj",
    )(h, kvnw, qnw, wk, wvT, wqT)


def _rel_bucket_np(n):
    n = np.asarray(n)
    max_exact = REL_BUCKETS // 2
    nf = np.maximum(n, 1).astype(np.float32)
    large = max_exact + (np.log(nf / np.float32(max_exact)) / np.float32(math.log(REL_MAX_DIST / max_exact))
                         * np.float32(REL_BUCKETS - max_exact)).astype(np.int32)
    large = np.minimum(large, REL_BUCKETS - 1)
    return np.where(n < max_exact, n, large).astype(np.int32)


def _bucket_tiles():
    i = np.arange(TK)[:, None]
    jd = np.arange(TK)[None, :]
    j = np.arange(TQ)[None, :]
    diag = np.where(jd - i >= 0, _rel_bucket_np(np.maximum(jd - i, 0)), -1)
    a = np.arange(BAND)[:, None]
    b = np.arange(BAND)[None, :]
    corner = _rel_bucket_np(b - a + BAND)
    m = np.arange(META_KEYS)[:, None] - (META_KEYS - N_META)
    dist0 = N_META + j - m
    meta0 = np.where(m >= 0, _rel_bucket_np(np.maximum(dist0, 0)), -1)
    meta_far = np.where(m >= 0, REL_BUCKETS - 1, -1) + 0 * j
    return (corner.astype(np.int32), diag.astype(np.int32),
            np.stack([meta0, meta_far]).astype(np.int32))


def _bias_kernel(tab_ref, corner_id_ref, diag_id_ref, meta_id_ref, corner_ref, diag_ref, meta_ref):
    h = pl.program_id(0)
    far = tab_ref[REL_BUCKETS - 1, h]

    def build(ids):
        out = jnp.where(ids < 0, NEG_INF, 0.0).astype(F32)
        for b in range(REL_BUCKETS - 1):
            out = jnp.where(ids == b, (tab_ref[b, h] - far) * LOG2E, out)
        return out

    corner_ref[0] = build(corner_id_ref[...])
    diag_ref[0] = build(diag_id_ref[...])
    meta_ref[0] = build(meta_id_ref[...])


def _bias_tiles(rel_bias):
    corner_ids, diag_ids, meta_ids = _bucket_tiles()
    return pl.pallas_call(
        _bias_kernel,
        grid=(DIFF_HEADS,),
        in_specs=[
            pl.BlockSpec(memory_space=pltpu.SMEM),
            _resident((BAND, BAND)),
            _resident((TK, TK)),
            _resident((2, META_KEYS, TQ)),
        ],
        out_specs=[
            pl.BlockSpec((1, BAND, BAND), lambda h: (h, 0, 0)),
            pl.BlockSpec((1, TK, TK), lambda h: (h, 0, 0)),
            pl.BlockSpec((1, 2, META_KEYS, TQ), lambda h: (h, 0, 0, 0)),
        ],
        out_shape=[
            jax.ShapeDtypeStruct((DIFF_HEADS, BAND, BAND), F32),
            jax.ShapeDtypeStruct((DIFF_HEADS, TK, TK), F32),
            jax.ShapeDtypeStruct((DIFF_HEADS, 2, META_KEYS, TQ), F32),
        ],
        compiler_params=_params("arbitrary"),
        name="rel_bias_tiles",
    )(rel_bias, jnp.asarray(corner_ids), jnp.asarray(diag_ids), jnp.asarray(meta_ids))


def _attn_kernel(qT_ref, k_ref, vT_ref, corner_ref, diag_ref, meta_ref, lam_ref, sw_ref, o_ref,
                 s_ref, smax_ref, p_ref, alpha_ref, m_ref, acc_ref, *, n_real, lambda_init):
    qi = pl.program_id(2)
    qT = qT_ref[0]
    sub = lax.broadcasted_iota(jnp.int32, (DIFF_VDIM, 1), 0)
    q_maps = (jnp.where(sub < DIFF_HEAD_DIM, qT, jnp.zeros_like(qT)),
              jnp.where(sub >= DIFF_HEAD_DIM, qT, jnp.zeros_like(qT)))
    all_groups = tuple(range(TQ // MXU_N))

    def groups_from(lane0):
        return tuple(qg for qg in all_groups if qg * MXU_N >= lane0)

    def lanes(qg):
        return slice(qg * MXU_N, (qg + 1) * MXU_N)

    def v_ext(lo, n):
        return jnp.concatenate([vT_ref[0, :, pl.ds(lo, n)],
                                jnp.ones((V_ROWS - DIFF_VDIM, n), BF16)], axis=0)

    def key_lo(step_idx):
        if isinstance(step_idx, int):
            return max(step_idx, 0) * TK
        return pl.multiple_of(jnp.maximum(step_idx, 0) * TK, TK)

    def scores_into(slot, lo, groups):
        k_blk = k_ref[pl.ds(lo, TK), :]
        for c in range(2):
            for qg in groups:
                s = jnp.dot(k_blk, q_maps[c][:, lanes(qg)], preferred_element_type=F32)
                s_ref[slot, c, :, lanes(qg)] = s
                smax_ref[slot, c, :, lanes(qg)] = jnp.max(s, axis=0, keepdims=True)

    def add_corner(slot, lane0):
        for c in range(2):
            blk = (slot, c, slice(TK - BAND, TK), slice(lane0, lane0 + BAND))
            s_ref[blk] = s_ref[blk] + corner_ref[0]
            qs = slice(lane0, lane0 + MXU_N)
            smax_ref[slot, c, :, qs] = jnp.max(s_ref[slot, c, :, qs], axis=0, keepdims=True)

    def accumulate(slot, lo, groups):
        vx = v_ext(lo, TK)
        for c in range(2):
            for qg in groups:
                qs = lanes(qg)
                acc_ref[c, :, qs] = (alpha_ref[slot, c, :, qs] * acc_ref[c, :, qs]
                                     + jnp.dot(vx, p_ref[slot, c, :, qs], preferred_element_type=F32))

    def step(slot, idx, groups=all_groups, diag_lane0=None, nxt=all_groups, lag=all_groups):
        if nxt:
            scores_into(1 - slot, key_lo(idx - 1), nxt)
        for c in range(2):
            for qg in groups:
                qs = lanes(qg)
                s = s_ref[slot, c, :, qs]
                on_diag = diag_lane0 is not None and 0 <= qg * MXU_N - diag_lane0 < TK
                if on_diag:
                    s = s + diag_ref[0, :, qg * MXU_N - diag_lane0:(qg + 1) * MXU_N - diag_lane0]
                    s_max = jnp.max(s, axis=0, keepdims=True)
                else:
                    s_max = smax_ref[slot, c, :, qs]
                m_old = m_ref[c, :, qs]
                m_new = jnp.maximum(m_old, s_max)
                alpha_ref[slot, c, :, qs] = jnp.exp2(m_old - m_new)
                p_ref[slot, c, :, qs] = jnp.exp2(s - m_new).astype(BF16)
                m_ref[c, :, qs] = m_new
        if lag:
            accumulate(1 - slot, key_lo(idx + 1), lag)

    meta_lo = n_real + CHUNK - META_KEYS
    k_m = k_ref[meta_lo:meta_lo + META_KEYS, :]
    vx_m = v_ext(meta_lo, META_KEYS)
    bias_m = meta_ref[0, jnp.minimum(qi, 1)]
    for c in range(2):
        s = jnp.dot(k_m, q_maps[c], preferred_element_type=F32) + bias_m
        m_c = jnp.max(s, axis=0, keepdims=True)
        m_ref[c] = m_c
        acc_ref[c] = jnp.dot(vx_m, jnp.exp2(s - m_c).astype(BF16), preferred_element_type=F32)

    first = KEY_STEPS * qi
    scores_into(0, key_lo(first + KEY_STEPS - 1), groups_from((KEY_STEPS - 1) * TK))
    for d in range(KEY_STEPS - 1, -1, -1):
        slot = (KEY_STEPS - 1 - d) % 2
        if d < KEY_STEPS - 1:
            add_corner(slot, (d + 1) * TK)
        step(slot, first + d, groups_from(d * TK), diag_lane0=d * TK,
             nxt=groups_from((d - 1) * TK) if d > 0 else all_groups,
             lag=groups_from((d + 1) * TK) if d < KEY_STEPS - 1 else ())

    @pl.when(qi > 0)
    def _():
        add_corner(0, 0)

        def pair(i, carry):
            idx = first - 1 - 2 * i
            step(0, idx)
            step(1, idx - 1)
            return carry

        lax.fori_loop(0, first // 2 - 1, pair, 0)
        step(0, 1)
        step(1, 0, nxt=())

    accumulate(1, key_lo(0), all_groups)

    lv = lam_ref[...]
    lam = (jnp.exp(jnp.sum(lv[0:1] * lv[1:2], axis=-1, keepdims=True))
           - jnp.exp(jnp.sum(lv[2:3] * lv[3:4], axis=-1, keepdims=True)) + lambda_init)
    o = [acc_ref[c, 0:DIFF_VDIM, :] / acc_ref[c, DIFF_VDIM:DIFF_VDIM + 1, :] for c in range(2)]
    o = o[0] - lam * o[1]
    o = o * lax.rsqrt(jnp.mean(o * o, axis=0, keepdims=True) + EPS)
    o = o * sw_ref[...] * (1.0 - lambda_init)
    o_ref[...] = o.T.astype(BF16)


def _attention(qT, k, vT, corner, diag, meta, lamv, sw, *, batch, n_real, lp, lambda_init):
    nq = n_real // TQ
    return pl.pallas_call(
        functools.partial(_attn_kernel, n_real=n_real, lambda_init=lambda_init),
        grid=(batch, DIFF_HEADS, nq),
        in_specs=[
            pl.BlockSpec((1, DIFF_VDIM, TQ), lambda b, h, i: (b, h, i)),
            pl.BlockSpec((lp, DIFF_VDIM), lambda b, h, i: (b, h)),
            pl.BlockSpec((1, DIFF_VDIM, lp), lambda b, h, i: (b, h, 0)),
            pl.BlockSpec((1, BAND, BAND), lambda b, h, i: (h, 0, 0)),
            pl.BlockSpec((1, TK, TK), lambda b, h, i: (h, 0, 0)),
            pl.BlockSpec((1, 2, META_KEYS, TQ), lambda b, h, i: (h, 0, 0, 0)),
            _resident((4, DIFF_HEAD_DIM)),
            _resident((DIFF_VDIM, 1)),
        ],
        out_specs=pl.BlockSpec((TQ, DIFF_VDIM), lambda b, h, i: (b * nq + i, h)),
        out_shape=jax.ShapeDtypeStruct((batch * n_real, D_MODEL), BF16),
        scratch_shapes=[
            pltpu.VMEM((2, 2, TK, TQ), F32),
            pltpu.VMEM((2, 2, 1, TQ), F32),
            pltpu.VMEM((2, 2, TK, TQ), BF16),
            pltpu.VMEM((2, 2, 1, TQ), F32),
            pltpu.VMEM((2, 1, TQ), F32),
            pltpu.VMEM((2, V_ROWS, TQ), F32),
        ],
        compiler_params=_params("parallel", "parallel", "arbitrary"),
        name="diff_attention",
    )(qT, k, vT, corner, diag, meta, lamv, sw)


def kernel(x, meta_tokens, norm_w, ssm_in_w, ssm_conv_w, ssm_conv_b, ssm_dt_bias, ssm_a_log, ssm_d,
           ssm_norm_w, ssm_out_w, kv_norm_w, w_kv, w_q, lam_q1, lam_k1, lam_q2, lam_k2, subln_w,
           w_o, rel_bias, ffn_w_gu, ffn_w_down, final_norm_w):
    batch, n_real, d = x.shape
    assert d == D_MODEL and n_real % TQ == 0 and n_real % ROW_TILE == 0 and KEY_STEPS % 2 == 0
    assert norm_w.shape[0] == 2 and ssm_in_w.shape[0] == 1 and w_q.shape[0] == 1
    lp = n_real + CHUNK
    n_pad = CHUNK - N_META
    n_chunks = lp // CHUNK
    rows = batch * lp
    assert rows % ROW_TILE == 0
    row2 = lambda v: v.reshape(1, -1).astype(F32)

    tail = jnp.concatenate([jnp.zeros((n_pad, D_MODEL), F32), meta_tokens.astype(F32)], axis=0)
    h0 = jnp.concatenate([x, jnp.broadcast_to(tail[None], (batch, CHUNK, D_MODEL))], axis=1)
    h0 = h0.reshape(rows, D_MODEL)

    in_w = ssm_in_w[0]
    w_zx = in_w[:, :D_INNER + CONV_DIM].astype(BF16)
    w_dt = jnp.pad(in_w[:, D_INNER + CONV_DIM:].astype(F32), ((0, 0), (0, LANES - SSM_HEADS)))
    w_dt_hi = w_dt.astype(BF16)
    w_dt_lo = (w_dt - w_dt_hi.astype(F32)).astype(BF16)
    z, xc, dt, dtT = _in_proj(h0, row2(norm_w[0, 0]), w_zx, w_dt_hi, w_dt_lo, ssm_conv_w[0].astype(F32),
                              row2(ssm_conv_b[0]), batch=batch, n_chunks=n_chunks)
    y = _ssd(xc, dt, dtT, row2(ssm_dt_bias[0]), ssm_dt_bias[0].reshape(-1, 1),
             row2(ssm_a_log[0]), ssm_a_log[0].reshape(-1, 1),
             row2(jnp.repeat(ssm_d[0], SSM_HEADDIM)), batch=batch, n_chunks=n_chunks, n_pad=n_pad)
    gu0 = ffn_w_gu[0].astype(BF16)
    h2 = _ssd_out_ffn(y, z, row2(ssm_norm_w[0]), ssm_out_w[0].astype(BF16), h0, row2(norm_w[0, 1]),
                      gu0[:, :FFN_HIDDEN], gu0[:, FFN_HIDDEN:], ffn_w_down[0].astype(BF16))

    layer = 1
    lambda_init = 0.8 - 0.6 * math.exp(-0.3 * layer)
    k, vT, qT = _kvq(h2, row2(kv_norm_w), row2(norm_w[1, 0]), w_kv[:, :D_MODEL].astype(BF16),
                     w_kv[:, D_MODEL:].T.astype(BF16), w_q[0].T.astype(BF16), batch=batch, lp=lp)
    corner, diag, meta = _bias_tiles(rel_bias.astype(F32))
    lamv = jnp.stack([lam_q1[0], lam_k1[0], lam_q2[0], lam_k2[0]]).astype(F32)
    attn = _attention(qT, k, vT, corner, diag, meta, lamv, subln_w[0].reshape(-1, 1).astype(F32),
                      batch=batch, n_real=n_real, lp=lp, lambda_init=lambda_init)
    gu1 = ffn_w_gu[1].astype(BF16)
    out = _attn_out_ffn(attn, w_o[0].astype(BF16), h2.reshape(batch, lp, D_MODEL), row2(norm_w[1, 1]),
                        gu1[:, :FFN_HIDDEN], gu1[:, FFN_HIDDEN:], ffn_w_down[1].astype(BF16),
                        row2(final_norm_w), batch=batch, n_real=n_real)
    return out.reshape(batch, n_real, D_MODEL)
```

```python
import functools
import math

import numpy as np
import jax
import jax.numpy as jnp
from jax import lax
from jax.experimental import pallas as pl
from jax.experimental.pallas import tpu as pltpu

F32 = jnp.float32
BF16 = jnp.bfloat16

D_MODEL = 1024
N_META = 16
EPS = 1e-6
NEG_INF = -1e30
SSM_HEADDIM = 64
SSM_HEADS = 32
SSM_GROUPS = 8
SSM_HPG = SSM_HEADS // SSM_GROUPS
SSM_STATE = 128
SSM_CONV = 4
D_INNER = SSM_HEADS * SSM_HEADDIM
GROUP_W = D_INNER // SSM_GROUPS
CONV_DIM = D_INNER + 2 * SSM_GROUPS * SSM_STATE
DIFF_HEADS = 8
DIFF_HEAD_DIM = 64
DIFF_VDIM = 2 * DIFF_HEAD_DIM
REL_BUCKETS = 32
REL_MAX_DIST = 128
FFN_HIDDEN = 2816

SUBLANES = 8
LANES = 128
CHUNK = 256
ROW_TILE = 512
CONV_TILE = 512
MXU_N = 256
TK = 512
KEY_STEPS = 4
TQ = KEY_STEPS * TK
BAND = REL_MAX_DIST
META_KEYS = 128
V_ROWS = DIFF_VDIM + 16
LOG2E = math.log2(math.e)
VMEM_LIMIT = 56 * 1024 * 1024


def _resident(shape):
    nd = len(shape)
    return pl.BlockSpec(shape, lambda *_: (0,) * nd, pipeline_mode=pl.Buffered(1))


def _params(*sem):
    return pltpu.CompilerParams(dimension_semantics=sem, vmem_limit_bytes=VMEM_LIMIT)


def _rmsnorm(x, w):
    return x * lax.rsqrt(jnp.mean(x * x, axis=-1, keepdims=True) + EPS) * w


def _softplus(v):
    return jnp.maximum(v, 0.0) + jnp.log(1.0 + jnp.exp(-jnp.abs(v)))


def _silu(v):
    return v * jax.nn.sigmoid(v)


def _split3(v):
    hi = v.astype(BF16)
    r1 = v - hi.astype(F32)
    mid = r1.astype(BF16)
    lo = (r1 - mid.astype(F32)).astype(BF16)
    return hi, mid, lo


_NT = (((1,), (1,)), ((), ()))


def _in_proj_kernel(x_ref, nw_ref, w_ref, wdh_ref, wdl_ref, cw_ref, cb_ref,
                    z_ref, xc_ref, dt_ref, dtT_ref, xe_ref):
    Q = CHUNK
    c = pl.program_id(1)

    @pl.when(c == 0)
    def _():
        xe_ref[0:SUBLANES, :] = jnp.zeros((SUBLANES, CONV_DIM), F32)

    xn = _rmsnorm(x_ref[...], nw_ref[...])
    xb = xn.astype(BF16)
    z_tile = CONV_TILE * D_INNER // CONV_DIM
    for t in range(CONV_DIM // CONV_TILE):
        c0 = t * CONV_TILE
        cs = slice(c0, c0 + CONV_TILE)
        pre = jnp.dot(xb, w_ref[:, D_INNER + c0:D_INNER + c0 + CONV_TILE], preferred_element_type=F32)
        xe_ref[SUBLANES:SUBLANES + Q, cs] = pre
        acc = cb_ref[:, cs] + cw_ref[SSM_CONV - 1:SSM_CONV, cs] * pre
        for k in range(SSM_CONV - 1):
            off = SUBLANES - (SSM_CONV - 1) + k
            acc = acc + cw_ref[k:k + 1, cs] * xe_ref[off:off + Q, cs]
        xc_ref[:, cs] = _silu(acc).astype(BF16)
        zs = slice(t * z_tile, (t + 1) * z_tile)
        z_ref[:, zs] = jnp.dot(xb, w_ref[:, zs], preferred_element_type=F32).astype(BF16)
    xe_ref[0:SUBLANES, :] = xe_ref[Q:Q + SUBLANES, :]
    x_lo = (xn - xb.astype(F32)).astype(BF16)
    dt = (jnp.dot(xb, wdh_ref[...], preferred_element_type=F32)
          + jnp.dot(x_lo, wdh_ref[...], preferred_element_type=F32)
          + jnp.dot(xb, wdl_ref[...], preferred_element_type=F32))
    dt_ref[...] = dt[:, 0:SSM_HEADS]
    dtT_ref[...] = dt.T[0:SSM_HEADS, :]


def _scan_block(n_chunks):
    return lambda b, c: b * n_chunks + (c + n_chunks - 1) % n_chunks


def _in_proj(h2d, nw, w_zx, w_dt_hi, w_dt_lo, cw, cb, *, batch, n_chunks):
    rows = h2d.shape[0]
    blk = _scan_block(n_chunks)
    return pl.pallas_call(
        _in_proj_kernel,
        grid=(batch, n_chunks),
        in_specs=[
            pl.BlockSpec((CHUNK, D_MODEL), lambda b, c: (blk(b, c), 0)),
            _resident((1, D_MODEL)),
            _resident((D_MODEL, D_INNER + CONV_DIM)),
            _resident((D_MODEL, LANES)),
            _resident((D_MODEL, LANES)),
            _resident((SSM_CONV, CONV_DIM)),
            _resident((1, CONV_DIM)),
        ],
        out_specs=[
            pl.BlockSpec((CHUNK, D_INNER), lambda b, c: (blk(b, c), 0)),
            pl.BlockSpec((CHUNK, CONV_DIM), lambda b, c: (blk(b, c), 0)),
            pl.BlockSpec((CHUNK, SSM_HEADS), lambda b, c: (blk(b, c), 0)),
            pl.BlockSpec((SSM_HEADS, CHUNK), lambda b, c: (0, blk(b, c))),
        ],
        out_shape=[
            jax.ShapeDtypeStruct((rows, D_INNER), BF16),
            jax.ShapeDtypeStruct((rows, CONV_DIM), BF16),
            jax.ShapeDtypeStruct((rows, SSM_HEADS), F32),
            jax.ShapeDtypeStruct((SSM_HEADS, rows), F32),
        ],
        scratch_shapes=[pltpu.VMEM((CHUNK + 2 * SUBLANES, CONV_DIM), F32)],
        compiler_params=_params("arbitrary", "arbitrary"),
        name="in_proj",
    )(h2d, nw, w_zx, w_dt_hi, w_dt_lo, cw, cb)


def _ssd_kernel(xc_ref, dt_ref, dtT_ref, dtb_ref, dtbT_ref, alog_ref, alogT_ref,
                dsk_ref, y_ref, state_ref, *, n_pad):
    Q = CHUNK
    c = pl.program_id(1)

    @pl.when(c == 0)
    def _():
        state_ref[...] = jnp.zeros_like(state_ref)

    first_valid = jnp.where(c == 0, n_pad, 0)
    row = lax.broadcasted_iota(jnp.int32, (Q, 1), 0)
    col = lax.broadcasted_iota(jnp.int32, (1, Q), 1)
    dtc = jnp.where(row >= first_valid, _softplus(dt_ref[...] + dtb_ref[...]), 0.0)
    dtr = jnp.where(col >= first_valid, _softplus(dtT_ref[...] + dtbT_ref[...]), 0.0)
    dac = dtc * (-jnp.exp(alog_ref[...]))
    dar = dtr * (-jnp.exp(alogT_ref[...]))
    ii = lax.broadcasted_iota(jnp.int32, (Q, Q), 0)
    jj = lax.broadcasted_iota(jnp.int32, (Q, Q), 1)
    tri = ii >= jj
    tri_b = tri.astype(BF16)
    tri_bt = (ii <= jj).astype(BF16)
    ac2 = sum(jnp.dot(tri_b, t, preferred_element_type=F32) for t in _split3(dac)) * LOG2E
    ar2 = sum(jnp.dot(t, tri_bt, preferred_element_type=F32) for t in _split3(dar)) * LOG2E
    ar2_dt = ar2 - jnp.where(dtr > 0.0, jnp.log2(dtr), NEG_INF)
    a_last2 = ar2[:, Q - 1:Q]
    w_state = jnp.exp2(a_last2 - ar2_dt)
    e_last = jnp.exp2(a_last2)
    eacs_c = jnp.exp2(ac2)
    lane = lax.broadcasted_iota(jnp.int32, (1, GROUP_W), 1)

    for g in range(SSM_GROUPS):
        b_g = xc_ref[:, D_INNER + g * SSM_STATE:D_INNER + (g + 1) * SSM_STATE]
        c_g = xc_ref[:, D_INNER + (SSM_GROUPS + g) * SSM_STATE:D_INNER + (SSM_GROUPS + g + 1) * SSM_STATE]
        x_g = xc_ref[:, g * GROUP_W:(g + 1) * GROUP_W]
        cb = lax.dot_general(c_g, b_g, _NT, preferred_element_type=F32)
        b_t = b_g.astype(F32).T
        s_old = state_ref[g]
        y_off = jnp.dot(c_g, s_old.astype(BF16), preferred_element_type=F32)
        y_diag = jnp.zeros((Q, GROUP_W), F32)
        s_add = jnp.zeros((SSM_STATE, GROUP_W), F32)
        scale = jnp.zeros((Q, GROUP_W), F32)
        sdec = jnp.zeros((1, GROUP_W), F32)
        for r in range(SSM_HPG):
            h = g * SSM_HPG + r
            decay_dt = jnp.exp2(jnp.where(tri, ac2[:, h:h + 1] - ar2_dt[h:h + 1, :], NEG_INF))
            w = (cb * decay_dt).astype(BF16)
            in_head = (lane >= r * SSM_HEADDIM) & (lane < (r + 1) * SSM_HEADDIM)
            x_r = jnp.where(in_head, x_g, jnp.zeros_like(x_g))
            y_diag = y_diag + jnp.dot(w, x_r, preferred_element_type=F32)
            s_add = s_add + jnp.dot((b_t * w_state[h:h + 1, :]).astype(BF16), x_r,
                                    preferred_element_type=F32)
            scale = jnp.where(in_head, eacs_c[:, h:h + 1], scale)
            sdec = jnp.where(in_head, e_last[h:h + 1, :], sdec)
        state_ref[g] = s_old * sdec + s_add
        gs = slice(g * GROUP_W, (g + 1) * GROUP_W)
        y_ref[:, gs] = (y_diag + y_off * scale + x_g.astype(F32) * dsk_ref[:, gs]).astype(BF16)


def _ssd(xc, dt, dtT, dtb, dtbT, alog, alogT, dsk, *, batch, n_chunks, n_pad):
    rows = xc.shape[0]
    blk = _scan_block(n_chunks)
    return pl.pallas_call(
        functools.partial(_ssd_kernel, n_pad=n_pad),
        grid=(batch, n_chunks),
        in_specs=[
            pl.BlockSpec((CHUNK, CONV_DIM), lambda b, c: (blk(b, c), 0)),
            pl.BlockSpec((CHUNK, SSM_HEADS), lambda b, c: (blk(b, c), 0)),
            pl.BlockSpec((SSM_HEADS, CHUNK), lambda b, c: (0, blk(b, c))),
            _resident((1, SSM_HEADS)),
            _resident((SSM_HEADS, 1)),
            _resident((1, SSM_HEADS)),
            _resident((SSM_HEADS, 1)),
            _resident((1, D_INNER)),
        ],
        out_specs=pl.BlockSpec((CHUNK, D_INNER), lambda b, c: (blk(b, c), 0)),
        out_shape=jax.ShapeDtypeStruct((rows, D_INNER), BF16),
        scratch_shapes=[pltpu.VMEM((SSM_GROUPS, SSM_STATE, GROUP_W), F32)],
        compiler_params=_params("arbitrary", "arbitrary"),
        name="ssd_scan",
    )(xc, dt, dtT, dtb, dtbT, alog, alogT, dsk)


def _swiglu_residual(x, nw_ref, wgu_ref, wd_ref):
    xb = _rmsnorm(x, nw_ref[...]).astype(BF16)
    gate_val = jnp.dot(xb, wgu_ref[...], preferred_element_type=F32)
    act = (_silu(gate_val[:, :FFN_HIDDEN]) * gate_val[:, FFN_HIDDEN:]).astype(BF16)
    return x + jnp.dot(act, wd_ref[...], preferred_element_type=F32)


def _ssd_out_ffn_kernel(y_ref, z_ref, gnw_ref, wo_ref, r_ref, nw_ref, wgu_ref, wd_ref, o_ref):
    parts = []
    for g in range(SSM_GROUPS):
        gs = slice(g * GROUP_W, (g + 1) * GROUP_W)
        y = y_ref[:, gs].astype(F32) * _silu(z_ref[:, gs].astype(F32))
        parts.append(_rmsnorm(y, gnw_ref[:, gs]).astype(BF16))
    a = jnp.concatenate(parts, axis=-1)
    h = r_ref[...] + jnp.dot(a, wo_ref[...], preferred_element_type=F32)
    o_ref[...] = _swiglu_residual(h, nw_ref, wgu_ref, wd_ref)


def _ssd_out_ffn(y, z, gnw, wo, res, nw, wgu, wd):
    rows = y.shape[0]
    tile = lambda w: pl.BlockSpec((ROW_TILE, w), lambda i: (i, 0))
    return pl.pallas_call(
        _ssd_out_ffn_kernel,
        grid=(rows // ROW_TILE,),
        in_specs=[
            tile(D_INNER), tile(D_INNER), _resident((1, D_INNER)), _resident((D_INNER, D_MODEL)),
            tile(D_MODEL), _resident((1, D_MODEL)), _resident((D_MODEL, 2 * FFN_HIDDEN)),
            _resident((FFN_HIDDEN, D_MODEL)),
        ],
        out_specs=tile(D_MODEL),
        out_shape=jax.ShapeDtypeStruct((rows, D_MODEL), F32),
        compiler_params=_params("parallel"),
        name="ssd_out_ffn",
    )(y, z, gnw, wo, res, nw, wgu, wd)


def _attn_out_ffn_kernel(a_ref, wo_ref, r_ref, nw_ref, wgu_ref, wd_ref, fnw_ref, o_ref):
    h = r_ref[0] + jnp.dot(a_ref[...], wo_ref[...], preferred_element_type=F32)
    o_ref[...] = _rmsnorm(_swiglu_residual(h, nw_ref, wgu_ref, wd_ref), fnw_ref[...])


def _attn_out_ffn(a, wo, res3, nw, wgu, wd, fnw, *, batch, n_real):
    tiles = n_real // ROW_TILE
    return pl.pallas_call(
        _attn_out_ffn_kernel,
        grid=(batch, tiles),
        in_specs=[
            pl.BlockSpec((ROW_TILE, D_MODEL), lambda b, i: (b * tiles + i, 0)),
            _resident((D_MODEL, D_MODEL)),
            pl.BlockSpec((1, ROW_TILE, D_MODEL), lambda b, i: (b, i, 0)),
            _resident((1, D_MODEL)), _resident((D_MODEL, 2 * FFN_HIDDEN)),
            _resident((FFN_HIDDEN, D_MODEL)), _resident((1, D_MODEL)),
        ],
        out_specs=pl.BlockSpec((ROW_TILE, D_MODEL), lambda b, i: (b * tiles + i, 0)),
        out_shape=jax.ShapeDtypeStruct((batch * n_real, D_MODEL), F32),
        compiler_params=_params("parallel", "parallel"),
        name="attn_out_ffn",
    )(a, wo, res3, nw, wgu, wd, fnw)


def _kvq_kernel(h_ref, kvnw_ref, qnw_ref, wk_ref, wvT_ref, wqT_ref, k_ref, vT_ref, qT_ref):
    x = h_ref[...]
    inv = lax.rsqrt(jnp.mean(x * x, axis=-1, keepdims=True) + EPS)
    xkv = (x * inv * kvnw_ref[...]).astype(BF16)
    xq = (x * inv * qnw_ref[...]).astype(BF16)
    k_ref[...] = jnp.dot(xkv, wk_ref[...], preferred_element_type=F32).astype(BF16)
    vT_ref[0] = lax.dot_general(wvT_ref[...], xkv, _NT, preferred_element_type=F32).astype(BF16)
    qT = lax.dot_general(wqT_ref[...], xq, _NT, preferred_element_type=F32)
    qT_ref[0] = (qT * (DIFF_HEAD_DIM ** -0.5 * LOG2E)).astype(BF16)


def _kvq(h, kvnw, qnw, wk, wvT, wqT, *, batch, lp):
    tiles = lp // CHUNK
    return pl.pallas_call(
        _kvq_kernel,
        grid=(batch, tiles),
        in_specs=[
            pl.BlockSpec((CHUNK, D_MODEL), lambda b, i: (b * tiles + i, 0)),
            _resident((1, D_MODEL)),
            _resident((1, D_MODEL)),
            _resident((D_MODEL, D_MODEL)),
            _resident((D_MODEL, D_MODEL)),
            _resident((D_MODEL, D_MODEL)),
        ],
        out_specs=[
            pl.BlockSpec((CHUNK, D_MODEL), lambda b, i: (b * tiles + i, 0)),
            pl.BlockSpec((1, D_MODEL, CHUNK), lambda b, i: (b, 0, i)),
            pl.BlockSpec((1, D_MODEL, CHUNK), lambda b, i: (b, 0, i)),
        ],
        out_shape=[
            jax.ShapeDtypeStruct((batch * lp, D_MODEL), BF16),
            jax.ShapeDtypeStruct((batch, D_MODEL, lp), BF16),
            jax.ShapeDtypeStruct((batch, D_MODEL, lp), BF16),
        ],
        compiler_params=_params("parallel", "parallel"),
        name="kvq_proj",
    )(h, kvnw, qnw, wk, wvT, wqT)


def _rel_bucket_np(n):
    n = np.asarray(n)
    max_exact = REL_BUCKETS // 2
    nf = np.maximum(n, 1).astype(np.float32)
    large = max_exact + (np.log(nf / np.float32(max_exact)) / np.float32(math.log(REL_MAX_DIST / max_exact))
                         * np.float32(REL_BUCKETS - max_exact)).astype(np.int32)
    large = np.minimum(large, REL_BUCKETS - 1)
    return np.where(n < max_exact, n, large).astype(np.int32)


def _bucket_tiles():
    i = np.arange(TK)[:, None]
    jd = np.arange(TK)[None, :]
    j = np.arange(BAND)[None, :]
    diag = np.where(jd - i >= 0, _rel_bucket_np(np.maximum(jd - i, 0)), -1)
    a = np.arange(BAND)[:, None]
    b = np.arange(BAND)[None, :]
    corner = _rel_bucket_np(b - a + BAND)
    m = np.arange(META_KEYS)[:, None] - (META_KEYS - N_META)
    dist0 = N_META + j - m
    meta0 = np.where(m >= 0, _rel_bucket_np(np.maximum(dist0, 0)), -1)
    meta_far = np.where(m >= 0, REL_BUCKETS - 1, -1) + 0 * j
    return (corner.astype(np.int32), diag.astype(np.int32),
            np.stack([meta0, meta_far]).astype(np.int32))


def _bias_kernel(tab_ref, corner_id_ref, diag_id_ref, meta_id_ref, corner_ref, diag_ref, meta_ref):
    h = pl.program_id(0)
    far = tab_ref[REL_BUCKETS - 1, h]

    def build(ids):
        out = jnp.where(ids < 0, NEG_INF, 0.0).astype(F32)
        for b in range(REL_BUCKETS - 1):
            out = jnp.where(ids == b, (tab_ref[b, h] - far) * LOG2E, out)
        return out

    corner_ref[0] = build(corner_id_ref[...])
    diag_ref[0] = build(diag_id_ref[...])
    meta_ref[0] = build(meta_id_ref[...])


def _bias_tiles(rel_bias):
    corner_ids, diag_ids, meta_ids = _bucket_tiles()
    return pl.pallas_call(
        _bias_kernel,
        grid=(DIFF_HEADS,),
        in_specs=[
            pl.BlockSpec(memory_space=pltpu.SMEM),
            _resident((BAND, BAND)),
            _resident((TK, TK)),
            _resident((2, META_KEYS, BAND)),
        ],
        out_specs=[
            pl.BlockSpec((1, BAND, BAND), lambda h: (h, 0, 0)),
            pl.BlockSpec((1, TK, TK), lambda h: (h, 0, 0)),
            pl.BlockSpec((1, 2, META_KEYS, BAND), lambda h: (h, 0, 0, 0)),
        ],
        out_shape=[
            jax.ShapeDtypeStruct((DIFF_HEADS, BAND, BAND), F32),
            jax.ShapeDtypeStruct((DIFF_HEADS, TK, TK), F32),
            jax.ShapeDtypeStruct((DIFF_HEADS, 2, META_KEYS, BAND), F32),
        ],
        compiler_params=_params("arbitrary"),
        name="rel_bias_tiles",
    )(rel_bias, jnp.asarray(corner_ids), jnp.asarray(diag_ids), jnp.asarray(meta_ids))


def _attn_kernel(qT_ref, k_ref, vT_ref, corner_ref, diag_ref, meta_ref, lam_ref, sw_ref, o_ref,
                 s_ref, smax_ref, p_ref, alpha_ref, m_ref, acc_ref, *, n_real, lambda_init):
    qi = pl.program_id(2)
    qT = qT_ref[0]
    sub = lax.broadcasted_iota(jnp.int32, (DIFF_VDIM, 1), 0)
    q_maps = (jnp.where(sub < DIFF_HEAD_DIM, qT, jnp.zeros_like(qT)),
              jnp.where(sub >= DIFF_HEAD_DIM, qT, jnp.zeros_like(qT)))
    all_groups = tuple(range(TQ // MXU_N))

    def groups_from(lane0):
        return tuple(qg for qg in all_groups if qg * MXU_N >= lane0)

    def lanes(qg):
        return slice(qg * MXU_N, (qg + 1) * MXU_N)

    def v_ext(lo, n):
        return jnp.concatenate([vT_ref[0, :, pl.ds(lo, n)],
                                jnp.ones((V_ROWS - DIFF_VDIM, n), BF16)], axis=0)

    def key_lo(step_idx):
        if isinstance(step_idx, int):
            return max(step_idx, 0) * TK
        return pl.multiple_of(jnp.maximum(step_idx, 0) * TK, TK)

    def scores_into(slot, lo, groups):
        k_blk = k_ref[pl.ds(lo, TK), :]
        for c in range(2):
            for qg in groups:
                s = jnp.dot(k_blk, q_maps[c][:, lanes(qg)], preferred_element_type=F32)
                s_ref[slot, c, :, lanes(qg)] = s
                smax_ref[slot, c, :, lanes(qg)] = jnp.max(s, axis=0, keepdims=True)

    def add_corner(slot, lane0):
        for c in range(2):
            blk = (slot, c, slice(TK - BAND, TK), slice(lane0, lane0 + BAND))
            s_ref[blk] = s_ref[blk] + corner_ref[0]
            qs = slice(lane0, lane0 + MXU_N)
            smax_ref[slot, c, :, qs] = jnp.max(s_ref[slot, c, :, qs], axis=0, keepdims=True)

    def accumulate(slot, lo, groups):
        vx = v_ext(lo, TK)
        for c in range(2):
            for qg in groups:
                qs = lanes(qg)
                acc_ref[c, :, qs] = (alpha_ref[slot, c, :, qs] * acc_ref[c, :, qs]
                                     + jnp.dot(vx, p_ref[slot, c, :, qs], preferred_element_type=F32))

    def step(slot, idx, groups=all_groups, diag_lane0=None, nxt=all_groups, lag=all_groups):
        if nxt:
            scores_into(1 - slot, key_lo(idx - 1), nxt)
        for c in range(2):
            for qg in groups:
                qs = lanes(qg)
                s = s_ref[slot, c, :, qs]
                on_diag = diag_lane0 is not None and 0 <= qg * MXU_N - diag_lane0 < TK
                if on_diag:
                    s = s + diag_ref[0, :, qg * MXU_N - diag_lane0:(qg + 1) * MXU_N - diag_lane0]
                    s_max = jnp.max(s, axis=0, keepdims=True)
                else:
                    s_max = smax_ref[slot, c, :, qs]
                m_old = m_ref[c, :, qs]
                m_new = jnp.maximum(m_old, s_max)
                alpha_ref[slot, c, :, qs] = jnp.exp2(m_old - m_new)
                p_ref[slot, c, :, qs] = jnp.exp2(s - m_new).astype(BF16)
                m_ref[c, :, qs] = m_new
        if lag:
            accumulate(1 - slot, key_lo(idx + 1), lag)

    meta_lo = n_real + CHUNK - META_KEYS
    k_m = k_ref[meta_lo:meta_lo + META_KEYS, :]
    vx_m = v_ext(meta_lo, META_KEYS)
    band_m = meta_ref[0, jnp.minimum(qi, 1)]
    mask_m = meta_ref[0, 1, :, 0:1]
    for c in range(2):
        s = jnp.dot(k_m, q_maps[c], preferred_element_type=F32)
        s = jnp.concatenate([s[:, :BAND] + band_m, s[:, BAND:] + mask_m], axis=1)
        m_c = jnp.max(s, axis=0, keepdims=True)
        m_ref[c] = m_c
        acc_ref[c] = jnp.dot(vx_m, jnp.exp2(s - m_c).astype(BF16), preferred_element_type=F32)

    first = KEY_STEPS * qi
    scores_into(0, key_lo(first + KEY_STEPS - 1), groups_from((KEY_STEPS - 1) * TK))
    for d in range(KEY_STEPS - 1, -1, -1):
        slot = (KEY_STEPS - 1 - d) % 2
        if d < KEY_STEPS - 1:
            add_corner(slot, (d + 1) * TK)
        step(slot, first + d, groups_from(d * TK), diag_lane0=d * TK,
             nxt=groups_from((d - 1) * TK) if d > 0 else all_groups,
             lag=groups_from((d + 1) * TK) if d < KEY_STEPS - 1 else ())

    @pl.when(qi > 0)
    def _():
        add_corner(0, 0)

        def pair(i, carry):
            idx = first - 1 - 2 * i
            step(0, idx)
            step(1, idx - 1)
            return carry

        lax.fori_loop(0, first // 2 - 1, pair, 0)
        step(0, 1)
        step(1, 0, nxt=())

    accumulate(1, key_lo(0), all_groups)

    lv = lam_ref[...]
    lam = (jnp.exp(jnp.sum(lv[0:1] * lv[1:2], axis=-1, keepdims=True))
           - jnp.exp(jnp.sum(lv[2:3] * lv[3:4], axis=-1, keepdims=True)) + lambda_init)
    o = [acc_ref[c, 0:DIFF_VDIM, :] / acc_ref[c, DIFF_VDIM:DIFF_VDIM + 1, :] for c in range(2)]
    o = o[0] - lam * o[1]
    o = o * lax.rsqrt(jnp.mean(o * o, axis=0, keepdims=True) + EPS)
    o = o * sw_ref[...] * (1.0 - lambda_init)
    o_ref[...] = o.T.astype(BF16)


def _attention(qT, k, vT, corner, diag, meta, lamv, sw, *, batch, n_real, lp, lambda_init):
    nq = n_real // TQ
    return pl.pallas_call(
        functools.partial(_attn_kernel, n_real=n_real, lambda_init=lambda_init),
        grid=(batch, DIFF_HEADS, nq),
        in_specs=[
            pl.BlockSpec((1, DIFF_VDIM, TQ), lambda b, h, i: (b, h, i)),
            pl.BlockSpec((lp, DIFF_VDIM), lambda b, h, i: (b, h)),
            pl.BlockSpec((1, DIFF_VDIM, lp), lambda b, h, i: (b, h, 0)),
            pl.BlockSpec((1, BAND, BAND), lambda b, h, i: (h, 0, 0)),
            pl.BlockSpec((1, TK, TK), lambda b, h, i: (h, 0, 0)),
            pl.BlockSpec((1, 2, META_KEYS, BAND), lambda b, h, i: (h, 0, 0, 0)),
            _resident((4, DIFF_HEAD_DIM)),
            _resident((DIFF_VDIM, 1)),
        ],
        out_specs=pl.BlockSpec((TQ, DIFF_VDIM), lambda b, h, i: (b * nq + i, h)),
        out_shape=jax.ShapeDtypeStruct((batch * n_real, D_MODEL), BF16),
        scratch_shapes=[
            pltpu.VMEM((2, 2, TK, TQ), F32),
            pltpu.VMEM((2, 2, 1, TQ), F32),
            pltpu.VMEM((2, 2, TK, TQ), BF16),
            pltpu.VMEM((2, 2, 1, TQ), F32),
            pltpu.VMEM((2, 1, TQ), F32),
            pltpu.VMEM((2, V_ROWS, TQ), F32),
        ],
        compiler_params=_params("parallel", "parallel", "arbitrary"),
        name="diff_attention",
    )(qT, k, vT, corner, diag, meta, lamv, sw)


def kernel(x, meta_tokens, norm_w, ssm_in_w, ssm_conv_w, ssm_conv_b, ssm_dt_bias, ssm_a_log, ssm_d,
           ssm_norm_w, ssm_out_w, kv_norm_w, w_kv, w_q, lam_q1, lam_k1, lam_q2, lam_k2, subln_w,
           w_o, rel_bias, ffn_w_gu, ffn_w_down, final_norm_w):
    batch, n_real, d = x.shape
    assert d == D_MODEL and n_real % TQ == 0 and n_real % ROW_TILE == 0 and KEY_STEPS % 2 == 0
    assert norm_w.shape[0] == 2 and ssm_in_w.shape[0] == 1 and w_q.shape[0] == 1
    lp = n_real + CHUNK
    n_pad = CHUNK - N_META
    n_chunks = lp // CHUNK
    rows = batch * lp
    assert rows % ROW_TILE == 0
    row2 = lambda v: v.reshape(1, -1).astype(F32)

    tail = jnp.concatenate([jnp.zeros((n_pad, D_MODEL), F32), meta_tokens.astype(F32)], axis=0)
    h0 = jnp.concatenate([x, jnp.broadcast_to(tail[None], (batch, CHUNK, D_MODEL))], axis=1)
    h0 = h0.reshape(rows, D_MODEL)

    in_w = ssm_in_w[0]
    w_zx = in_w[:, :D_INNER + CONV_DIM].astype(BF16)
    w_dt = jnp.pad(in_w[:, D_INNER + CONV_DIM:].astype(F32), ((0, 0), (0, LANES - SSM_HEADS)))
    w_dt_hi = w_dt.astype(BF16)
    w_dt_lo = (w_dt - w_dt_hi.astype(F32)).astype(BF16)
    z, xc, dt, dtT = _in_proj(h0, row2(norm_w[0, 0]), w_zx, w_dt_hi, w_dt_lo, ssm_conv_w[0].astype(F32),
                              row2(ssm_conv_b[0]), batch=batch, n_chunks=n_chunks)
    y = _ssd(xc, dt, dtT, row2(ssm_dt_bias[0]), ssm_dt_bias[0].reshape(-1, 1),
             row2(ssm_a_log[0]), ssm_a_log[0].reshape(-1, 1),
             row2(jnp.repeat(ssm_d[0], SSM_HEADDIM)), batch=batch, n_chunks=n_chunks, n_pad=n_pad)
    h2 = _ssd_out_ffn(y, z, row2(ssm_norm_w[0]), ssm_out_w[0].astype(BF16), h0, row2(norm_w[0, 1]),
                      ffn_w_gu[0].astype(BF16), ffn_w_down[0].astype(BF16))

    layer = 1
    lambda_init = 0.8 - 0.6 * math.exp(-0.3 * layer)
    k, vT, qT = _kvq(h2, row2(kv_norm_w), row2(norm_w[1, 0]), w_kv[:, :D_MODEL].astype(BF16),
                     w_kv[:, D_MODEL:].T.astype(BF16), w_q[0].T.astype(BF16), batch=batch, lp=lp)
    corner, diag, meta = _bias_tiles(rel_bias.astype(F32))
    lamv = jnp.stack([lam_q1[0], lam_k1[0], lam_q2[0], lam_k2[0]]).astype(F32)
    attn = _attention(qT, k, vT, corner, diag, meta, lamv, subln_w[0].reshape(-1, 1).astype(F32),
                      batch=batch, n_real=n_real, lp=lp, lambda_init=lambda_init)
    out = _attn_out_ffn(attn, w_o[0].astype(BF16), h2.reshape(batch, lp, D_MODEL), row2(norm_w[1, 1]),
                        ffn_w_gu[1].astype(BF16), ffn_w_down[1].astype(BF16),
                        row2(final_norm_w), batch=batch, n_real=n_real)
    return out.reshape(batch, n_real, D_MODEL)
```

```python
import functools
import math

import numpy as np
import jax
import jax.numpy as jnp
from jax import lax
from jax.experimental import pallas as pl
from jax.experimental.pallas import tpu as pltpu

F32 = jnp.float32
BF16 = jnp.bfloat16

D_MODEL = 1024
N_META = 16
EPS = 1e-6
NEG_INF = -1e30
SSM_HEADDIM = 64
SSM_HEADS = 32
SSM_GROUPS = 8
SSM_HPG = SSM_HEADS // SSM_GROUPS
SSM_STATE = 128
SSM_CONV = 4
D_INNER = SSM_HEADS * SSM_HEADDIM
GROUP_W = D_INNER // SSM_GROUPS
CONV_DIM = D_INNER + 2 * SSM_GROUPS * SSM_STATE
DIFF_HEADS = 8
DIFF_HEAD_DIM = 64
DIFF_VDIM = 2 * DIFF_HEAD_DIM
REL_BUCKETS = 32
REL_MAX_DIST = 128
FFN_HIDDEN = 2816

SUBLANES = 8
LANES = 128
CHUNK = 256
TIME_BLOCKS = CHUNK // SUBLANES
ROW_TILE = 512
CONV_TILE = 512
MXU_N = 256
TK = 512
KEY_STEPS = 4
TQ = KEY_STEPS * TK
BAND = REL_MAX_DIST
META_KEYS = 128
V_ROWS = DIFF_VDIM + 16
LOG2E = math.log2(math.e)
VMEM_LIMIT = 56 * 1024 * 1024


def _resident(shape):
    nd = len(shape)
    return pl.BlockSpec(shape, lambda *_: (0,) * nd, pipeline_mode=pl.Buffered(1))


def _params(*sem):
    return pltpu.CompilerParams(dimension_semantics=sem, vmem_limit_bytes=VMEM_LIMIT)


def _rmsnorm(x, w):
    return x * lax.rsqrt(jnp.mean(x * x, axis=-1, keepdims=True) + EPS) * w


def _softplus(v):
    return jnp.maximum(v, 0.0) + jnp.log(1.0 + jnp.exp(-jnp.abs(v)))


def _silu(v):
    return v * jax.nn.sigmoid(v)


def _split3(v):
    hi = v.astype(BF16)
    r1 = v - hi.astype(F32)
    mid = r1.astype(BF16)
    lo = (r1 - mid.astype(F32)).astype(BF16)
    return hi, mid, lo


_NT = (((1,), (1,)), ((), ()))


def _chunk_time(r):
    return jnp.bitwise_and(r, SUBLANES - 1) * TIME_BLOCKS + jnp.right_shift(r, 3)


def _in_proj_kernel(x_ref, nw_ref, w_ref, wdh_ref, wdl_ref, cw_ref, cb_ref,
                    z_ref, xc_ref, dt_ref, dtT_ref, xe_ref, carry_ref):
    Q = CHUNK
    halo = (SSM_CONV - 1) * SUBLANES
    c = pl.program_id(1)

    @pl.when(c == 0)
    def _():
        carry_ref[...] = jnp.zeros_like(carry_ref)

    xn = _rmsnorm(x_ref[...], nw_ref[...])
    xb = xn.astype(BF16)
    x_lo = (xn - xb.astype(F32)).astype(BF16)
    ii = lax.broadcasted_iota(jnp.int32, (Q, Q), 0)
    jj = lax.broadcasted_iota(jnp.int32, (Q, Q), 1)
    perm = (_chunk_time(ii) == jj).astype(BF16)
    xb_t = jnp.dot(perm, xb, preferred_element_type=F32).astype(BF16)
    x_lo_t = jnp.dot(perm, x_lo, preferred_element_type=F32).astype(BF16)
    first_sub = lax.broadcasted_iota(jnp.int32, (SUBLANES, 1), 0) == 0
    z_tile = CONV_TILE * D_INNER // CONV_DIM
    for t in range(CONV_DIM // CONV_TILE):
        c0 = t * CONV_TILE
        cs = slice(c0, c0 + CONV_TILE)
        pre = jnp.dot(xb_t, w_ref[:, D_INNER + c0:D_INNER + c0 + CONV_TILE], preferred_element_type=F32)
        xe_ref[halo:halo + Q, cs] = pre
        for j in range(SSM_CONV - 1):
            blk = slice(j * SUBLANES, (j + 1) * SUBLANES)
            cur = pre[Q - halo + j * SUBLANES:Q - halo + (j + 1) * SUBLANES, :]
            xe_ref[blk, cs] = jnp.where(first_sub, pltpu.roll(carry_ref[blk, cs], 1, 0),
                                        pltpu.roll(cur, 1, 0))
            carry_ref[blk, cs] = cur
        acc = cb_ref[:, cs] + cw_ref[SSM_CONV - 1:SSM_CONV, cs] * pre
        for k in range(SSM_CONV - 1):
            off = halo - (SSM_CONV - 1 - k) * SUBLANES
            acc = acc + cw_ref[k:k + 1, cs] * xe_ref[off:off + Q, cs]
        xc_ref[:, cs] = _silu(acc).astype(BF16)
        zs = slice(t * z_tile, (t + 1) * z_tile)
        z_ref[:, zs] = jnp.dot(xb, w_ref[:, zs], preferred_element_type=F32).astype(BF16)
    dt = (jnp.dot(xb_t, wdh_ref[...], preferred_element_type=F32)
          + jnp.dot(x_lo_t, wdh_ref[...], preferred_element_type=F32)
          + jnp.dot(xb_t, wdl_ref[...], preferred_element_type=F32))
    dt_ref[...] = dt[:, 0:SSM_HEADS]
    dtT_ref[...] = dt.T[0:SSM_HEADS, :]


def _scan_block(n_chunks):
    return lambda b, c: b * n_chunks + (c + n_chunks - 1) % n_chunks


def _in_proj(h2d, nw, w_zx, w_dt_hi, w_dt_lo, cw, cb, *, batch, n_chunks):
    rows = h2d.shape[0]
    blk = _scan_block(n_chunks)
    return pl.pallas_call(
        _in_proj_kernel,
        grid=(batch, n_chunks),
        in_specs=[
            pl.BlockSpec((CHUNK, D_MODEL), lambda b, c: (blk(b, c), 0)),
            _resident((1, D_MODEL)),
            _resident((D_MODEL, D_INNER + CONV_DIM)),
            _resident((D_MODEL, LANES)),
            _resident((D_MODEL, LANES)),
            _resident((SSM_CONV, CONV_DIM)),
            _resident((1, CONV_DIM)),
        ],
        out_specs=[
            pl.BlockSpec((CHUNK, D_INNER), lambda b, c: (blk(b, c), 0)),
            pl.BlockSpec((CHUNK, CONV_DIM), lambda b, c: (blk(b, c), 0)),
            pl.BlockSpec((CHUNK, SSM_HEADS), lambda b, c: (blk(b, c), 0)),
            pl.BlockSpec((SSM_HEADS, CHUNK), lambda b, c: (0, blk(b, c))),
        ],
        out_shape=[
            jax.ShapeDtypeStruct((rows, D_INNER), BF16),
            jax.ShapeDtypeStruct((rows, CONV_DIM), BF16),
            jax.ShapeDtypeStruct((rows, SSM_HEADS), F32),
            jax.ShapeDtypeStruct((SSM_HEADS, rows), F32),
        ],
        scratch_shapes=[
            pltpu.VMEM((CHUNK + (SSM_CONV - 1) * SUBLANES, CONV_DIM), F32),
            pltpu.VMEM(((SSM_CONV - 1) * SUBLANES, CONV_DIM), F32),
        ],
        compiler_params=_params("arbitrary", "arbitrary"),
        name="in_proj",
    )(h2d, nw, w_zx, w_dt_hi, w_dt_lo, cw, cb)


def _ssd_kernel(xc_ref, dt_ref, dtT_ref, dtb_ref, dtbT_ref, alog_ref, alogT_ref,
                dsk_ref, y_ref, state_ref, causal_ref, tri_ref, *, n_pad):
    Q = CHUNK
    c = pl.program_id(1)

    @pl.when(c == 0)
    def _():
        state_ref[...] = jnp.zeros_like(state_ref)
        ti = _chunk_time(lax.broadcasted_iota(jnp.int32, (Q, Q), 0))
        tj = _chunk_time(lax.broadcasted_iota(jnp.int32, (Q, Q), 1))
        causal_ref[...] = jnp.where(ti >= tj, 0.0, NEG_INF)
        tri_ref[0] = (ti >= tj).astype(BF16)
        tri_ref[1] = (ti <= tj).astype(BF16)

    first_valid = jnp.where(c == 0, n_pad, 0)
    t_col = _chunk_time(lax.broadcasted_iota(jnp.int32, (Q, 1), 0))
    t_row = _chunk_time(lax.broadcasted_iota(jnp.int32, (1, Q), 1))
    dtc = jnp.where(t_col >= first_valid, _softplus(dt_ref[...] + dtb_ref[...]), 0.0)
    dtr = jnp.where(t_row >= first_valid, _softplus(dtT_ref[...] + dtbT_ref[...]), 0.0)
    dac = dtc * (-jnp.exp(alog_ref[...]))
    dar = dtr * (-jnp.exp(alogT_ref[...]))
    ac2 = sum(jnp.dot(tri_ref[0], t, preferred_element_type=F32) for t in _split3(dac)) * LOG2E
    ar2 = sum(jnp.dot(t, tri_ref[1], preferred_element_type=F32) for t in _split3(dar)) * LOG2E
    ar2_dt = ar2 - jnp.where(dtr > 0.0, jnp.log2(dtr), NEG_INF)
    a_last2 = ar2[:, Q - 1:Q]
    w_state = jnp.exp2(a_last2 - ar2_dt)
    e_last = jnp.exp2(a_last2)
    eacs_c = jnp.exp2(ac2)
    lane = lax.broadcasted_iota(jnp.int32, (1, GROUP_W), 1)

    for g in range(SSM_GROUPS):
        b_g = xc_ref[:, D_INNER + g * SSM_STATE:D_INNER + (g + 1) * SSM_STATE]
        c_g = xc_ref[:, D_INNER + (SSM_GROUPS + g) * SSM_STATE:D_INNER + (SSM_GROUPS + g + 1) * SSM_STATE]
        x_g = xc_ref[:, g * GROUP_W:(g + 1) * GROUP_W]
        cb = lax.dot_general(c_g, b_g, _NT, preferred_element_type=F32)
        b_t = b_g.astype(F32).T
        s_old = state_ref[g]
        y_off = jnp.dot(c_g, s_old.astype(BF16), preferred_element_type=F32)
        y_diag = jnp.zeros((Q, GROUP_W), F32)
        s_add = jnp.zeros((SSM_STATE, GROUP_W), F32)
        scale = jnp.zeros((Q, GROUP_W), F32)
        sdec = jnp.zeros((1, GROUP_W), F32)
        for r in range(SSM_HPG):
            h = g * SSM_HPG + r
            decay_dt = jnp.exp2(ac2[:, h:h + 1] - ar2_dt[h:h + 1, :] + causal_ref[...])
            w = (cb * decay_dt).astype(BF16)
            in_head = (lane >= r * SSM_HEADDIM) & (lane < (r + 1) * SSM_HEADDIM)
            x_r = jnp.where(in_head, x_g, jnp.zeros_like(x_g))
            y_diag = y_diag + jnp.dot(w, x_r, preferred_element_type=F32)
            s_add = s_add + jnp.dot((b_t * w_state[h:h + 1, :]).astype(BF16), x_r,
                                    preferred_element_type=F32)
            scale = jnp.where(in_head, eacs_c[:, h:h + 1], scale)
            sdec = jnp.where(in_head, e_last[h:h + 1, :], sdec)
        state_ref[g] = s_old * sdec + s_add
        gs = slice(g * GROUP_W, (g + 1) * GROUP_W)
        y_ref[:, gs] = (y_diag + y_off * scale + x_g.astype(F32) * dsk_ref[:, gs]).astype(BF16)


def _ssd(xc, dt, dtT, dtb, dtbT, alog, alogT, dsk, *, batch, n_chunks, n_pad):
    rows = xc.shape[0]
    blk = _scan_block(n_chunks)
    return pl.pallas_call(
        functools.partial(_ssd_kernel, n_pad=n_pad),
        grid=(batch, n_chunks),
        in_specs=[
            pl.BlockSpec((CHUNK, CONV_DIM), lambda b, c: (blk(b, c), 0)),
            pl.BlockSpec((CHUNK, SSM_HEADS), lambda b, c: (blk(b, c), 0)),
            pl.BlockSpec((SSM_HEADS, CHUNK), lambda b, c: (0, blk(b, c))),
            _resident((1, SSM_HEADS)),
            _resident((SSM_HEADS, 1)),
            _resident((1, SSM_HEADS)),
            _resident((SSM_HEADS, 1)),
            _resident((1, D_INNER)),
        ],
        out_specs=pl.BlockSpec((CHUNK, D_INNER), lambda b, c: (blk(b, c), 0)),
        out_shape=jax.ShapeDtypeStruct((rows, D_INNER), BF16),
        scratch_shapes=[
            pltpu.VMEM((SSM_GROUPS, SSM_STATE, GROUP_W), F32),
            pltpu.VMEM((CHUNK, CHUNK), F32),
            pltpu.VMEM((2, CHUNK, CHUNK), BF16),
        ],
        compiler_params=_params("arbitrary", "arbitrary"),
        name="ssd_scan",
    )(xc, dt, dtT, dtb, dtbT, alog, alogT, dsk)


def _swiglu_residual(x, nw_ref, wgu_ref, wd_ref):
    xb = _rmsnorm(x, nw_ref[...]).astype(BF16)
    gate_val = jnp.dot(xb, wgu_ref[...], preferred_element_type=F32)
    act = (_silu(gate_val[:, :FFN_HIDDEN]) * gate_val[:, FFN_HIDDEN:]).astype(BF16)
    return x + jnp.dot(act, wd_ref[...], preferred_element_type=F32)


def _ssd_out_ffn_kernel(y_ref, z_ref, gnw_ref, wo_ref, r_ref, nw_ref, wgu_ref, wd_ref, o_ref):
    ii = lax.broadcasted_iota(jnp.int32, (CHUNK, CHUNK), 0)
    jj = lax.broadcasted_iota(jnp.int32, (CHUNK, CHUNK), 1)
    unperm = (ii == _chunk_time(jj)).astype(BF16)
    parts = []
    for g in range(SSM_GROUPS):
        gs = slice(g * GROUP_W, (g + 1) * GROUP_W)
        y = jnp.concatenate(
            [jnp.dot(unperm, y_ref[r0:r0 + CHUNK, gs], preferred_element_type=F32)
             for r0 in range(0, ROW_TILE, CHUNK)], axis=0)
        y = y * _silu(z_ref[:, gs].astype(F32))
        parts.append(_rmsnorm(y, gnw_ref[:, gs]).astype(BF16))
    a = jnp.concatenate(parts, axis=-1)
    h = r_ref[...] + jnp.dot(a, wo_ref[...], preferred_element_type=F32)
    o_ref[...] = _swiglu_residual(h, nw_ref, wgu_ref, wd_ref)


def _ssd_out_ffn(y, z, gnw, wo, res, nw, wgu, wd):
    rows = y.shape[0]
    tile = lambda w: pl.BlockSpec((ROW_TILE, w), lambda i: (i, 0))
    return pl.pallas_call(
        _ssd_out_ffn_kernel,
        grid=(rows // ROW_TILE,),
        in_specs=[
            tile(D_INNER), tile(D_INNER), _resident((1, D_INNER)), _resident((D_INNER, D_MODEL)),
            tile(D_MODEL), _resident((1, D_MODEL)), _resident((D_MODEL, 2 * FFN_HIDDEN)),
            _resident((FFN_HIDDEN, D_MODEL)),
        ],
        out_specs=tile(D_MODEL),
        out_shape=jax.ShapeDtypeStruct((rows, D_MODEL), F32),
        compiler_params=_params("parallel"),
        name="ssd_out_ffn",
    )(y, z, gnw, wo, res, nw, wgu, wd)


def _attn_out_ffn_kernel(a_ref, wo_ref, r_ref, nw_ref, wgu_ref, wd_ref, fnw_ref, o_ref):
    h = r_ref[0] + jnp.dot(a_ref[...], wo_ref[...], preferred_element_type=F32)
    o_ref[...] = _rmsnorm(_swiglu_residual(h, nw_ref, wgu_ref, wd_ref), fnw_ref[...])


def _attn_out_ffn(a, wo, res3, nw, wgu, wd, fnw, *, batch, n_real):
    tiles = n_real // ROW_TILE
    return pl.pallas_call(
        _attn_out_ffn_kernel,
        grid=(batch, tiles),
        in_specs=[
            pl.BlockSpec((ROW_TILE, D_MODEL), lambda b, i: (b * tiles + i, 0)),
            _resident((D_MODEL, D_MODEL)),
            pl.BlockSpec((1, ROW_TILE, D_MODEL), lambda b, i: (b, i, 0)),
            _resident((1, D_MODEL)), _resident((D_MODEL, 2 * FFN_HIDDEN)),
            _resident((FFN_HIDDEN, D_MODEL)), _resident((1, D_MODEL)),
        ],
        out_specs=pl.BlockSpec((ROW_TILE, D_MODEL), lambda b, i: (b * tiles + i, 0)),
        out_shape=jax.ShapeDtypeStruct((batch * n_real, D_MODEL), F32),
        compiler_params=_params("parallel", "parallel"),
        name="attn_out_ffn",
    )(a, wo, res3, nw, wgu, wd, fnw)


def _kvq_kernel(h_ref, kvnw_ref, qnw_ref, wk_ref, wvT_ref, wqT_ref, k_ref, vT_ref, qT_ref):
    x = h_ref[...]
    inv = lax.rsqrt(jnp.mean(x * x, axis=-1, keepdims=True) + EPS)
    xkv = (x * inv * kvnw_ref[...]).astype(BF16)
    xq = (x * inv * qnw_ref[...]).astype(BF16)
    k_ref[...] = jnp.dot(xkv, wk_ref[...], preferred_element_type=F32).astype(BF16)
    vT_ref[0] = lax.dot_general(wvT_ref[...], xkv, _NT, preferred_element_type=F32).astype(BF16)
    qT = lax.dot_general(wqT_ref[...], xq, _NT, preferred_element_type=F32)
    qT_ref[0] = (qT * (DIFF_HEAD_DIM ** -0.5 * LOG2E)).astype(BF16)


def _kvq(h, kvnw, qnw, wk, wvT, wqT, *, batch, lp):
    tiles = lp // CHUNK
    return pl.pallas_call(
        _kvq_kernel,
        grid=(batch, tiles),
        in_specs=[
            pl.BlockSpec((CHUNK, D_MODEL), lambda b, i: (b * tiles + i, 0)),
            _resident((1, D_MODEL)),
            _resident((1, D_MODEL)),
            _resident((D_MODEL, D_MODEL)),
            _resident((D_MODEL, D_MODEL)),
            _resident((D_MODEL, D_MODEL)),
        ],
        out_specs=[
            pl.BlockSpec((CHUNK, D_MODEL), lambda b, i: (b * tiles + i, 0)),
            pl.BlockSpec((1, D_MODEL, CHUNK), lambda b, i: (b, 0, i)),
            pl.BlockSpec((1, D_MODEL, CHUNK), lambda b, i: (b, 0, i)),
        ],
        out_shape=[
            jax.ShapeDtypeStruct((batch * lp, D_MODEL), BF16),
            jax.ShapeDtypeStruct((batch, D_MODEL, lp), BF16),
            jax.ShapeDtypeStruct((batch, D_MODEL, lp), BF16),
        ],
        compiler_params=_params("parallel", "parallel"),
        name="kvq_proj",
    )(h, kvnw, qnw, wk, wvT, wqT)


def _rel_bucket_np(n):
    n = np.asarray(n)
    max_exact = REL_BUCKETS // 2
    nf = np.maximum(n, 1).astype(np.float32)
    large = max_exact + (np.log(nf / np.float32(max_exact)) / np.float32(math.log(REL_MAX_DIST / max_exact))
                         * np.float32(REL_BUCKETS - max_exact)).astype(np.int32)
    large = np.minimum(large, REL_BUCKETS - 1)
    return np.where(n < max_exact, n, large).astype(np.int32)


def _bucket_tiles():
    i = np.arange(TK)[:, None]
    jd = np.arange(TK)[None, :]
    j = np.arange(BAND)[None, :]
    diag = np.where(jd - i >= 0, _rel_bucket_np(np.maximum(jd - i, 0)), -1)
    a = np.arange(BAND)[:, None]
    b = np.arange(BAND)[None, :]
    corner = _rel_bucket_np(b - a + BAND)
    m = np.arange(META_KEYS)[:, None] - (META_KEYS - N_META)
    dist0 = N_META + j - m
    meta0 = np.where(m >= 0, _rel_bucket_np(np.maximum(dist0, 0)), -1)
    meta_far = np.where(m >= 0, REL_BUCKETS - 1, -1) + 0 * j
    return (corner.astype(np.int32), diag.astype(np.int32),
            np.stack([meta0, meta_far]).astype(np.int32))


def _bias_kernel(tab_ref, corner_id_ref, diag_id_ref, meta_id_ref, corner_ref, diag_ref, meta_ref):
    h = pl.program_id(0)
    far = tab_ref[REL_BUCKETS - 1, h]

    def build(ids):
        out = jnp.where(ids < 0, NEG_INF, 0.0).astype(F32)
        for b in range(REL_BUCKETS - 1):
            out = jnp.where(ids == b, (tab_ref[b, h] - far) * LOG2E, out)
        return out

    corner_ref[0] = build(corner_id_ref[...])
    diag_ref[0] = build(diag_id_ref[...])
    meta_ref[0] = build(meta_id_ref[...])


def _bias_tiles(rel_bias):
    corner_ids, diag_ids, meta_ids = _bucket_tiles()
    return pl.pallas_call(
        _bias_kernel,
        grid=(DIFF_HEADS,),
        in_specs=[
            pl.BlockSpec(memory_space=pltpu.SMEM),
            _resident((BAND, BAND)),
            _resident((TK, TK)),
            _resident((2, META_KEYS, BAND)),
        ],
        out_specs=[
            pl.BlockSpec((1, BAND, BAND), lambda h: (h, 0, 0)),
            pl.BlockSpec((1, TK, TK), lambda h: (h, 0, 0)),
            pl.BlockSpec((1, 2, META_KEYS, BAND), lambda h: (h, 0, 0, 0)),
        ],
        out_shape=[
            jax.ShapeDtypeStruct((DIFF_HEADS, BAND, BAND), F32),
            jax.ShapeDtypeStruct((DIFF_HEADS, TK, TK), F32),
            jax.ShapeDtypeStruct((DIFF_HEADS, 2, META_KEYS, BAND), F32),
        ],
        compiler_params=_params("arbitrary"),
        name="rel_bias_tiles",
    )(rel_bias, jnp.asarray(corner_ids), jnp.asarray(diag_ids), jnp.asarray(meta_ids))


def _attn_kernel(qT_ref, k_ref, vT_ref, corner_ref, diag_ref, meta_ref, lam_ref, sw_ref, o_ref,
                 s_ref, smax_ref, p_ref, alpha_ref, m_ref, acc_ref, *, n_real, lambda_init):
    qi = pl.program_id(2)
    qT = qT_ref[0]
    sub = lax.broadcasted_iota(jnp.int32, (DIFF_VDIM, 1), 0)
    q_maps = (jnp.where(sub < DIFF_HEAD_DIM, qT, jnp.zeros_like(qT)),
              jnp.where(sub >= DIFF_HEAD_DIM, qT, jnp.zeros_like(qT)))
    all_groups = tuple(range(TQ // MXU_N))

    def groups_from(lane0):
        return tuple(qg for qg in all_groups if qg * MXU_N >= lane0)

    def lanes(qg):
        return slice(qg * MXU_N, (qg + 1) * MXU_N)

    def v_ext(lo, n):
        return jnp.concatenate([vT_ref[0, :, pl.ds(lo, n)],
                                jnp.ones((V_ROWS - DIFF_VDIM, n), BF16)], axis=0)

    def key_lo(step_idx):
        if isinstance(step_idx, int):
            return max(step_idx, 0) * TK
        return pl.multiple_of(jnp.maximum(step_idx, 0) * TK, TK)

    def scores_into(slot, lo, groups):
        k_blk = k_ref[pl.ds(lo, TK), :]
        for c in range(2):
            for qg in groups:
                s = jnp.dot(k_blk, q_maps[c][:, lanes(qg)], preferred_element_type=F32)
                s_ref[slot, c, :, lanes(qg)] = s
                smax_ref[slot, c, :, lanes(qg)] = jnp.max(s, axis=0, keepdims=True)

    def add_corner(slot, lane0):
        for c in range(2):
            blk = (slot, c, slice(TK - BAND, TK), slice(lane0, lane0 + BAND))
            s_ref[blk] = s_ref[blk] + corner_ref[0]
            qs = slice(lane0, lane0 + MXU_N)
            smax_ref[slot, c, :, qs] = jnp.max(s_ref[slot, c, :, qs], axis=0, keepdims=True)

    def accumulate(slot, lo, groups):
        vx = v_ext(lo, TK)
        for c in range(2):
            for qg in groups:
                qs = lanes(qg)
                acc_ref[c, :, qs] = (alpha_ref[slot, c, :, qs] * acc_ref[c, :, qs]
                                     + jnp.dot(vx, p_ref[slot, c, :, qs], preferred_element_type=F32))

    def step(slot, idx, groups=all_groups, diag_lane0=None, nxt=all_groups, lag=all_groups):
        if nxt:
            scores_into(1 - slot, key_lo(idx - 1), nxt)
        for c in range(2):
            for qg in groups:
                qs = lanes(qg)
                s = s_ref[slot, c, :, qs]
                on_diag = diag_lane0 is not None and 0 <= qg * MXU_N - diag_lane0 < TK
                if on_diag:
                    s = s + diag_ref[0, :, qg * MXU_N - diag_lane0:(qg + 1) * MXU_N - diag_lane0]
                    s_max = jnp.max(s, axis=0, keepdims=True)
                else:
                    s_max = smax_ref[slot, c, :, qs]
                m_old = m_ref[c, :, qs]
                m_new = jnp.maximum(m_old, s_max)
                alpha_ref[slot, c, :, qs] = jnp.exp2(m_old - m_new)
                p_ref[slot, c, :, qs] = jnp.exp2(s - m_new).astype(BF16)
                m_ref[c, :, qs] = m_new
        if lag:
            accumulate(1 - slot, key_lo(idx + 1), lag)

    meta_lo = n_real + CHUNK - META_KEYS
    k_m = k_ref[meta_lo:meta_lo + META_KEYS, :]
    vx_m = v_ext(meta_lo, META_KEYS)
    band_m = meta_ref[0, jnp.minimum(qi, 1)]
    mask_m = meta_ref[0, 1, :, 0:1]
    for c in range(2):
        s = jnp.dot(k_m, q_maps[c], preferred_element_type=F32)
        s = jnp.concatenate([s[:, :BAND] + band_m, s[:, BAND:] + mask_m], axis=1)
        m_c = jnp.max(s, axis=0, keepdims=True)
        m_ref[c] = m_c
        acc_ref[c] = jnp.dot(vx_m, jnp.exp2(s - m_c).astype(BF16), preferred_element_type=F32)

    first = KEY_STEPS * qi
    scores_into(0, key_lo(first + KEY_STEPS - 1), groups_from((KEY_STEPS - 1) * TK))
    for d in range(KEY_STEPS - 1, -1, -1):
        slot = (KEY_STEPS - 1 - d) % 2
        if d < KEY_STEPS - 1:
            add_corner(slot, (d + 1) * TK)
        step(slot, first + d, groups_from(d * TK), diag_lane0=d * TK,
             nxt=groups_from((d - 1) * TK) if d > 0 else all_groups,
             lag=groups_from((d + 1) * TK) if d < KEY_STEPS - 1 else ())

    @pl.when(qi > 0)
    def _():
        add_corner(0, 0)

        def pair(i, carry):
            idx = first - 1 - 2 * i
            step(0, idx)
            step(1, idx - 1)
            return carry

        lax.fori_loop(0, first // 2 - 1, pair, 0)
        step(0, 1)
        step(1, 0, nxt=())

    accumulate(1, key_lo(0), all_groups)

    lv = lam_ref[...]
    lam = (jnp.exp(jnp.sum(lv[0:1] * lv[1:2], axis=-1, keepdims=True))
           - jnp.exp(jnp.sum(lv[2:3] * lv[3:4], axis=-1, keepdims=True)) + lambda_init)
    o = [acc_ref[c, 0:DIFF_VDIM, :] / acc_ref[c, DIFF_VDIM:DIFF_VDIM + 1, :] for c in range(2)]
    o = o[0] - lam * o[1]
    o = o * lax.rsqrt(jnp.mean(o * o, axis=0, keepdims=True) + EPS)
    o = o * sw_ref[...] * (1.0 - lambda_init)
    o_ref[...] = o.T.astype(BF16)


def _attention(qT, k, vT, corner, diag, meta, lamv, sw, *, batch, n_real, lp, lambda_init):
    nq = n_real // TQ
    return pl.pallas_call(
        functools.partial(_attn_kernel, n_real=n_real, lambda_init=lambda_init),
        grid=(batch, DIFF_HEADS, nq),
        in_specs=[
            pl.BlockSpec((1, DIFF_VDIM, TQ), lambda b, h, i: (b, h, i)),
            pl.BlockSpec((lp, DIFF_VDIM), lambda b, h, i: (b, h)),
            pl.BlockSpec((1, DIFF_VDIM, lp), lambda b, h, i: (b, h, 0)),
            pl.BlockSpec((1, BAND, BAND), lambda b, h, i: (h, 0, 0)),
            pl.BlockSpec((1, TK, TK), lambda b, h, i: (h, 0, 0)),
            pl.BlockSpec((1, 2, META_KEYS, BAND), lambda b, h, i: (h, 0, 0, 0)),
            _resident((4, DIFF_HEAD_DIM)),
            _resident((DIFF_VDIM, 1)),
        ],
        out_specs=pl.BlockSpec((TQ, DIFF_VDIM), lambda b, h, i: (b * nq + i, h)),
        out_shape=jax.ShapeDtypeStruct((batch * n_real, D_MODEL), BF16),
        scratch_shapes=[
            pltpu.VMEM((2, 2, TK, TQ), F32),
            pltpu.VMEM((2, 2, 1, TQ), F32),
            pltpu.VMEM((2, 2, TK, TQ), BF16),
            pltpu.VMEM((2, 2, 1, TQ), F32),
            pltpu.VMEM((2, 1, TQ), F32),
            pltpu.VMEM((2, V_ROWS, TQ), F32),
        ],
        compiler_params=_params("parallel", "parallel", "arbitrary"),
        name="diff_attention",
    )(qT, k, vT, corner, diag, meta, lamv, sw)


def kernel(x, meta_tokens, norm_w, ssm_in_w, ssm_conv_w, ssm_conv_b, ssm_dt_bias, ssm_a_log, ssm_d,
           ssm_norm_w, ssm_out_w, kv_norm_w, w_kv, w_q, lam_q1, lam_k1, lam_q2, lam_k2, subln_w,
           w_o, rel_bias, ffn_w_gu, ffn_w_down, final_norm_w):
    batch, n_real, d = x.shape
    assert d == D_MODEL and n_real % TQ == 0 and n_real % ROW_TILE == 0 and KEY_STEPS % 2 == 0
    assert norm_w.shape[0] == 2 and ssm_in_w.shape[0] == 1 and w_q.shape[0] == 1
    lp = n_real + CHUNK
    n_pad = CHUNK - N_META
    n_chunks = lp // CHUNK
    rows = batch * lp
    assert rows % ROW_TILE == 0
    row2 = lambda v: v.reshape(1, -1).astype(F32)

    tail = jnp.concatenate([jnp.zeros((n_pad, D_MODEL), F32), meta_tokens.astype(F32)], axis=0)
    h0 = jnp.concatenate([x, jnp.broadcast_to(tail[None], (batch, CHUNK, D_MODEL))], axis=1)
    h0 = h0.reshape(rows, D_MODEL)

    in_w = ssm_in_w[0]
    w_zx = in_w[:, :D_INNER + CONV_DIM].astype(BF16)
    w_dt = jnp.pad(in_w[:, D_INNER + CONV_DIM:].astype(F32), ((0, 0), (0, LANES - SSM_HEADS)))
    w_dt_hi = w_dt.astype(BF16)
    w_dt_lo = (w_dt - w_dt_hi.astype(F32)).astype(BF16)
    z, xc, dt, dtT = _in_proj(h0, row2(norm_w[0, 0]), w_zx, w_dt_hi, w_dt_lo, ssm_conv_w[0].astype(F32),
                              row2(ssm_conv_b[0]), batch=batch, n_chunks=n_chunks)
    y = _ssd(xc, dt, dtT, row2(ssm_dt_bias[0]), ssm_dt_bias[0].reshape(-1, 1),
             row2(ssm_a_log[0]), ssm_a_log[0].reshape(-1, 1),
             row2(jnp.repeat(ssm_d[0], SSM_HEADDIM)), batch=batch, n_chunks=n_chunks, n_pad=n_pad)
    h2 = _ssd_out_ffn(y, z, row2(ssm_norm_w[0]), ssm_out_w[0].astype(BF16), h0, row2(norm_w[0, 1]),
                      ffn_w_gu[0].astype(BF16), ffn_w_down[0].astype(BF16))

    layer = 1
    lambda_init = 0.8 - 0.6 * math.exp(-0.3 * layer)
    k, vT, qT = _kvq(h2, row2(kv_norm_w), row2(norm_w[1, 0]), w_kv[:, :D_MODEL].astype(BF16),
                     w_kv[:, D_MODEL:].T.astype(BF16), w_q[0].T.astype(BF16), batch=batch, lp=lp)
    corner, diag, meta = _bias_tiles(rel_bias.astype(F32))
    lamv = jnp.stack([lam_q1[0], lam_k1[0], lam_q2[0], lam_k2[0]]).astype(F32)
    attn = _attention(qT, k, vT, corner, diag, meta, lamv, subln_w[0].reshape(-1, 1).astype(F32),
                      batch=batch, n_real=n_real, lp=lp, lambda_init=lambda_init)
    out = _attn_out_ffn(attn, w_o[0].astype(BF16), h2.reshape(batch, lp, D_MODEL), row2(norm_w[1, 1]),
                        ffn_w_gu[1].astype(BF16), ffn_w_down[1].astype(BF16),
                        row2(final_norm_w), batch=batch, n_real=n_real)
    return out.reshape(batch, n_real, D_MODEL)
```

```python
import functools
import math

import numpy as np
import jax
import jax.numpy as jnp
from jax import lax
from jax.experimental import pallas as pl
from jax.experimental.pallas import tpu as pltpu

F32 = jnp.float32
BF16 = jnp.bfloat16

D_MODEL = 1024
N_META = 16
EPS = 1e-6
NEG_INF = -1e30
SSM_HEADDIM = 64
SSM_HEADS = 32
SSM_GROUPS = 8
SSM_HPG = SSM_HEADS // SSM_GROUPS
SSM_STATE = 128
SSM_CONV = 4
D_INNER = SSM_HEADS * SSM_HEADDIM
GROUP_W = D_INNER // SSM_GROUPS
CONV_DIM = D_INNER + 2 * SSM_GROUPS * SSM_STATE
DIFF_HEADS = 8
DIFF_HEAD_DIM = 64
DIFF_VDIM = 2 * DIFF_HEAD_DIM
REL_BUCKETS = 32
REL_MAX_DIST = 128
FFN_HIDDEN = 2816

SUBLANES = 8
LANES = 128
CHUNK = 256
TIME_BLOCKS = CHUNK // SUBLANES
ROW_TILE = 512
CONV_TILE = 512
MXU_N = 256
TK = 512
KEY_STEPS = 4
TQ = KEY_STEPS * TK
BAND = REL_MAX_DIST
META_KEYS = 128
V_ROWS = DIFF_VDIM + 16
LOG2E = math.log2(math.e)
VMEM_LIMIT = 56 * 1024 * 1024


def _resident(shape):
    nd = len(shape)
    return pl.BlockSpec(shape, lambda *_: (0,) * nd, pipeline_mode=pl.Buffered(1))


def _params(*sem):
    return pltpu.CompilerParams(dimension_semantics=sem, vmem_limit_bytes=VMEM_LIMIT)


def _rmsnorm(x, w):
    return x * lax.rsqrt(jnp.mean(x * x, axis=-1, keepdims=True) + EPS) * w


def _softplus(v):
    return jnp.maximum(v, 0.0) + jnp.log(1.0 + jnp.exp(-jnp.abs(v)))


def _silu(v):
    return v * jax.nn.sigmoid(v)


def _split3(v):
    hi = v.astype(BF16)
    r1 = v - hi.astype(F32)
    mid = r1.astype(BF16)
    lo = (r1 - mid.astype(F32)).astype(BF16)
    return hi, mid, lo


_NT = (((1,), (1,)), ((), ()))


def _chunk_time(r):
    return jnp.bitwise_and(r, SUBLANES - 1) * TIME_BLOCKS + jnp.right_shift(r, SUBLANES.bit_length() - 1)


def _in_proj_kernel(x_ref, tail_ref, nw_ref, w_ref, wdh_ref, wdl_ref, cw_ref, cb_ref,
                    z_ref, xc_ref, dt_ref, dtT_ref, xe_ref, carry_ref):
    Q = CHUNK
    halo = (SSM_CONV - 1) * SUBLANES
    c = pl.program_id(1)

    @pl.when(c == 0)
    def _():
        carry_ref[...] = jnp.zeros_like(carry_ref)

    xn = _rmsnorm(jnp.where(c == 0, tail_ref[...], x_ref[0]), nw_ref[...])
    xb = xn.astype(BF16)
    x_lo = (xn - xb.astype(F32)).astype(BF16)
    ii = lax.broadcasted_iota(jnp.int32, (Q, Q), 0)
    jj = lax.broadcasted_iota(jnp.int32, (Q, Q), 1)
    perm = (_chunk_time(ii) == jj).astype(BF16)
    xb_t = jnp.dot(perm, xb, preferred_element_type=F32).astype(BF16)
    x_lo_t = jnp.dot(perm, x_lo, preferred_element_type=F32).astype(BF16)
    first_sub = lax.broadcasted_iota(jnp.int32, (SUBLANES, 1), 0) == 0
    z_tile = CONV_TILE * D_INNER // CONV_DIM
    for t in range(CONV_DIM // CONV_TILE):
        c0 = t * CONV_TILE
        cs = slice(c0, c0 + CONV_TILE)
        pre = jnp.dot(xb_t, w_ref[:, D_INNER + c0:D_INNER + c0 + CONV_TILE], preferred_element_type=F32)
        xe_ref[halo:halo + Q, cs] = pre
        for j in range(SSM_CONV - 1):
            blk = slice(j * SUBLANES, (j + 1) * SUBLANES)
            cur = pre[Q - halo + j * SUBLANES:Q - halo + (j + 1) * SUBLANES, :]
            xe_ref[blk, cs] = jnp.where(first_sub, pltpu.roll(carry_ref[blk, cs], 1, 0),
                                        pltpu.roll(cur, 1, 0))
            carry_ref[blk, cs] = cur
        acc = cb_ref[:, cs] + cw_ref[SSM_CONV - 1:SSM_CONV, cs] * pre
        for k in range(SSM_CONV - 1):
            off = halo - (SSM_CONV - 1 - k) * SUBLANES
            acc = acc + cw_ref[k:k + 1, cs] * xe_ref[off:off + Q, cs]
        xc_ref[:, cs] = _silu(acc).astype(BF16)
        zs = slice(t * z_tile, (t + 1) * z_tile)
        z_ref[:, zs] = jnp.dot(xb, w_ref[:, zs], preferred_element_type=F32).astype(BF16)
    dt = (jnp.dot(xb_t, wdh_ref[...], preferred_element_type=F32)
          + jnp.dot(x_lo_t, wdh_ref[...], preferred_element_type=F32)
          + jnp.dot(xb_t, wdl_ref[...], preferred_element_type=F32))
    dt_ref[...] = dt[:, 0:SSM_HEADS]
    dtT_ref[...] = dt.T[0:SSM_HEADS, :]


def _scan_block(n_chunks):
    return lambda b, c: b * n_chunks + (c + n_chunks - 1) % n_chunks


def _in_proj(x3, tail, nw, w_zx, w_dt_hi, w_dt_lo, cw, cb, *, batch, n_chunks):
    rows = batch * n_chunks * CHUNK
    blk = _scan_block(n_chunks)
    return pl.pallas_call(
        _in_proj_kernel,
        grid=(batch, n_chunks),
        in_specs=[
            pl.BlockSpec((1, CHUNK, D_MODEL), lambda b, c: (b, jnp.maximum(c - 1, 0), 0)),
            _resident((CHUNK, D_MODEL)),
            _resident((1, D_MODEL)),
            _resident((D_MODEL, D_INNER + CONV_DIM)),
            _resident((D_MODEL, LANES)),
            _resident((D_MODEL, LANES)),
            _resident((SSM_CONV, CONV_DIM)),
            _resident((1, CONV_DIM)),
        ],
        out_specs=[
            pl.BlockSpec((CHUNK, D_INNER), lambda b, c: (blk(b, c), 0)),
            pl.BlockSpec((CHUNK, CONV_DIM), lambda b, c: (blk(b, c), 0)),
            pl.BlockSpec((CHUNK, SSM_HEADS), lambda b, c: (blk(b, c), 0)),
            pl.BlockSpec((SSM_HEADS, CHUNK), lambda b, c: (0, blk(b, c))),
        ],
        out_shape=[
            jax.ShapeDtypeStruct((rows, D_INNER), BF16),
            jax.ShapeDtypeStruct((rows, CONV_DIM), BF16),
            jax.ShapeDtypeStruct((rows, SSM_HEADS), F32),
            jax.ShapeDtypeStruct((SSM_HEADS, rows), F32),
        ],
        scratch_shapes=[
            pltpu.VMEM((CHUNK + (SSM_CONV - 1) * SUBLANES, CONV_DIM), F32),
            pltpu.VMEM(((SSM_CONV - 1) * SUBLANES, CONV_DIM), F32),
        ],
        compiler_params=_params("arbitrary", "arbitrary"),
        name="in_proj",
    )(x3, tail, nw, w_zx, w_dt_hi, w_dt_lo, cw, cb)


def _ssd_kernel(xc_ref, dt_ref, dtT_ref, dtb_ref, dtbT_ref, alog_ref, alogT_ref,
                dsk_ref, y_ref, state_ref, causal_ref, tri_ref, *, n_pad):
    Q = CHUNK
    c = pl.program_id(1)

    @pl.when(c == 0)
    def _():
        state_ref[...] = jnp.zeros_like(state_ref)
        ti = _chunk_time(lax.broadcasted_iota(jnp.int32, (Q, Q), 0))
        tj = _chunk_time(lax.broadcasted_iota(jnp.int32, (Q, Q), 1))
        causal_ref[...] = jnp.where(ti >= tj, 0.0, NEG_INF)
        tri_ref[0] = (ti >= tj).astype(BF16)
        tri_ref[1] = (ti <= tj).astype(BF16)

    first_valid = jnp.where(c == 0, n_pad, 0)
    t_col = _chunk_time(lax.broadcasted_iota(jnp.int32, (Q, 1), 0))
    t_row = _chunk_time(lax.broadcasted_iota(jnp.int32, (1, Q), 1))
    dtc = jnp.where(t_col >= first_valid, _softplus(dt_ref[...] + dtb_ref[...]), 0.0)
    dtr = jnp.where(t_row >= first_valid, _softplus(dtT_ref[...] + dtbT_ref[...]), 0.0)
    dac = dtc * (-jnp.exp(alog_ref[...]))
    dar = dtr * (-jnp.exp(alogT_ref[...]))
    ac2 = sum(jnp.dot(tri_ref[0], t, preferred_element_type=F32) for t in _split3(dac)) * LOG2E
    ar2 = sum(jnp.dot(t, tri_ref[1], preferred_element_type=F32) for t in _split3(dar)) * LOG2E
    ar2_dt = ar2 - jnp.where(dtr > 0.0, jnp.log2(dtr), NEG_INF)
    a_last2 = ar2[:, Q - 1:Q]
    w_state = jnp.exp2(a_last2 - ar2_dt)
    e_last = jnp.exp2(a_last2)
    eacs_c = jnp.exp2(ac2)
    lane = lax.broadcasted_iota(jnp.int32, (1, GROUP_W), 1)

    for g in range(SSM_GROUPS):
        b_g = xc_ref[:, D_INNER + g * SSM_STATE:D_INNER + (g + 1) * SSM_STATE]
        c_g = xc_ref[:, D_INNER + (SSM_GROUPS + g) * SSM_STATE:D_INNER + (SSM_GROUPS + g + 1) * SSM_STATE]
        x_g = xc_ref[:, g * GROUP_W:(g + 1) * GROUP_W]
        cb = lax.dot_general(c_g, b_g, _NT, preferred_element_type=F32)
        b_t = b_g.astype(F32).T
        s_old = state_ref[g]
        y_off = jnp.dot(c_g, s_old.astype(BF16), preferred_element_type=F32)
        y_diag = jnp.zeros((Q, GROUP_W), F32)
        s_add = jnp.zeros((SSM_STATE, GROUP_W), F32)
        scale = jnp.zeros((Q, GROUP_W), F32)
        sdec = jnp.zeros((1, GROUP_W), F32)
        for r in range(SSM_HPG):
            h = g * SSM_HPG + r
            decay_dt = jnp.exp2(ac2[:, h:h + 1] - ar2_dt[h:h + 1, :] + causal_ref[...])
            w = (cb * decay_dt).astype(BF16)
            in_head = (lane >= r * SSM_HEADDIM) & (lane < (r + 1) * SSM_HEADDIM)
            x_r = jnp.where(in_head, x_g, jnp.zeros_like(x_g))
            y_diag = y_diag + jnp.dot(w, x_r, preferred_element_type=F32)
            s_add = s_add + jnp.dot((b_t * w_state[h:h + 1, :]).astype(BF16), x_r,
                                    preferred_element_type=F32)
            scale = jnp.where(in_head, eacs_c[:, h:h + 1], scale)
            sdec = jnp.where(in_head, e_last[h:h + 1, :], sdec)
        state_ref[g] = s_old * sdec + s_add
        gs = slice(g * GROUP_W, (g + 1) * GROUP_W)
        y_ref[:, gs] = (y_diag + y_off * scale + x_g.astype(F32) * dsk_ref[:, gs]).astype(BF16)


def _ssd(xc, dt, dtT, dtb, dtbT, alog, alogT, dsk, *, batch, n_chunks, n_pad):
    rows = xc.shape[0]
    blk = _scan_block(n_chunks)
    return pl.pallas_call(
        functools.partial(_ssd_kernel, n_pad=n_pad),
        grid=(batch, n_chunks),
        in_specs=[
            pl.BlockSpec((CHUNK, CONV_DIM), lambda b, c: (blk(b, c), 0)),
            pl.BlockSpec((CHUNK, SSM_HEADS), lambda b, c: (blk(b, c), 0)),
            pl.BlockSpec((SSM_HEADS, CHUNK), lambda b, c: (0, blk(b, c))),
            _resident((1, SSM_HEADS)),
            _resident((SSM_HEADS, 1)),
            _resident((1, SSM_HEADS)),
            _resident((SSM_HEADS, 1)),
            _resident((1, D_INNER)),
        ],
        out_specs=pl.BlockSpec((CHUNK, D_INNER), lambda b, c: (blk(b, c), 0)),
        out_shape=jax.ShapeDtypeStruct((rows, D_INNER), BF16),
        scratch_shapes=[
            pltpu.VMEM((SSM_GROUPS, SSM_STATE, GROUP_W), F32),
            pltpu.VMEM((CHUNK, CHUNK), F32),
            pltpu.VMEM((2, CHUNK, CHUNK), BF16),
        ],
        compiler_params=_params("arbitrary", "arbitrary"),
        name="ssd_scan",
    )(xc, dt, dtT, dtb, dtbT, alog, alogT, dsk)


def _swiglu_residual(x, nw_ref, wgu_ref, wd_ref):
    xb = _rmsnorm(x, nw_ref[...]).astype(BF16)
    gate_val = jnp.dot(xb, wgu_ref[...], preferred_element_type=F32)
    act = (_silu(gate_val[:, :FFN_HIDDEN]) * gate_val[:, FFN_HIDDEN:]).astype(BF16)
    return x + jnp.dot(act, wd_ref[...], preferred_element_type=F32)


def _ssd_out_ffn_kernel(y_ref, z_ref, gnw_ref, wo_ref, x_ref, tail_ref, nw_ref, wgu_ref, wd_ref, o_ref,
                        *, n_chunks):
    ii = lax.broadcasted_iota(jnp.int32, (CHUNK, CHUNK), 0)
    jj = lax.broadcasted_iota(jnp.int32, (CHUNK, CHUNK), 1)
    unperm = (ii == _chunk_time(jj)).astype(BF16)
    parts = []
    for g in range(SSM_GROUPS):
        gs = slice(g * GROUP_W, (g + 1) * GROUP_W)
        y = jnp.dot(unperm, y_ref[:, gs], preferred_element_type=F32)
        y = y * _silu(z_ref[:, gs].astype(F32))
        parts.append(_rmsnorm(y, gnw_ref[:, gs]).astype(BF16))
    a = jnp.concatenate(parts, axis=-1)
    res = jnp.where(pl.program_id(1) == n_chunks - 1, tail_ref[...], x_ref[0])
    h = res + jnp.dot(a, wo_ref[...], preferred_element_type=F32)
    o_ref[...] = _swiglu_residual(h, nw_ref, wgu_ref, wd_ref)


def _ssd_out_ffn(y, z, gnw, wo, x3, tail, nw, wgu, wd, *, batch, n_chunks):
    rows = y.shape[0]
    tile = lambda w: pl.BlockSpec((CHUNK, w), lambda b, j: (b * n_chunks + j, 0))
    return pl.pallas_call(
        functools.partial(_ssd_out_ffn_kernel, n_chunks=n_chunks),
        grid=(batch, n_chunks),
        in_specs=[
            tile(D_INNER), tile(D_INNER), _resident((1, D_INNER)), _resident((D_INNER, D_MODEL)),
            pl.BlockSpec((1, CHUNK, D_MODEL), lambda b, j: (b, jnp.minimum(j, n_chunks - 2), 0)),
            _resident((CHUNK, D_MODEL)),
            _resident((1, D_MODEL)), _resident((D_MODEL, 2 * FFN_HIDDEN)),
            _resident((FFN_HIDDEN, D_MODEL)),
        ],
        out_specs=tile(D_MODEL),
        out_shape=jax.ShapeDtypeStruct((rows, D_MODEL), F32),
        compiler_params=_params("parallel", "parallel"),
        name="ssd_out_ffn",
    )(y, z, gnw, wo, x3, tail, nw, wgu, wd)


def _attn_out_ffn_kernel(a_ref, wo_ref, r_ref, nw_ref, wgu_ref, wd_ref, fnw_ref, o_ref):
    h = r_ref[0] + jnp.dot(a_ref[...], wo_ref[...], preferred_element_type=F32)
    o_ref[...] = _rmsnorm(_swiglu_residual(h, nw_ref, wgu_ref, wd_ref), fnw_ref[...])


def _attn_out_ffn(a, wo, res3, nw, wgu, wd, fnw, *, batch, n_real):
    tiles = n_real // ROW_TILE
    return pl.pallas_call(
        _attn_out_ffn_kernel,
        grid=(batch, tiles),
        in_specs=[
            pl.BlockSpec((ROW_TILE, D_MODEL), lambda b, i: (b * tiles + i, 0)),
            _resident((D_MODEL, D_MODEL)),
            pl.BlockSpec((1, ROW_TILE, D_MODEL), lambda b, i: (b, i, 0)),
            _resident((1, D_MODEL)), _resident((D_MODEL, 2 * FFN_HIDDEN)),
            _resident((FFN_HIDDEN, D_MODEL)), _resident((1, D_MODEL)),
        ],
        out_specs=pl.BlockSpec((ROW_TILE, D_MODEL), lambda b, i: (b * tiles + i, 0)),
        out_shape=jax.ShapeDtypeStruct((batch * n_real, D_MODEL), F32),
        compiler_params=_params("parallel", "parallel"),
        name="attn_out_ffn",
    )(a, wo, res3, nw, wgu, wd, fnw)


def _kvq_kernel(h_ref, kvnw_ref, qnw_ref, wk_ref, wvT_ref, wqT_ref, k_ref, vT_ref, qT_ref):
    x = h_ref[...]
    inv = lax.rsqrt(jnp.mean(x * x, axis=-1, keepdims=True) + EPS)
    xkv = (x * inv * kvnw_ref[...]).astype(BF16)
    xq = (x * inv * qnw_ref[...]).astype(BF16)
    k_ref[...] = jnp.dot(xkv, wk_ref[...], preferred_element_type=F32).astype(BF16)
    vT_ref[0] = lax.dot_general(wvT_ref[...], xkv, _NT, preferred_element_type=F32).astype(BF16)
    qT = lax.dot_general(wqT_ref[...], xq, _NT, preferred_element_type=F32)
    qT_ref[0] = (qT * (DIFF_HEAD_DIM ** -0.5 * LOG2E)).astype(BF16)


def _kvq(h, kvnw, qnw, wk, wvT, wqT, *, batch, lp):
    tiles = lp // CHUNK
    return pl.pallas_call(
        _kvq_kernel,
        grid=(batch, tiles),
        in_specs=[
            pl.BlockSpec((CHUNK, D_MODEL), lambda b, i: (b * tiles + i, 0)),
            _resident((1, D_MODEL)),
            _resident((1, D_MODEL)),
            _resident((D_MODEL, D_MODEL)),
            _resident((D_MODEL, D_MODEL)),
            _resident((D_MODEL, D_MODEL)),
        ],
        out_specs=[
            pl.BlockSpec((CHUNK, D_MODEL), lambda b, i: (b * tiles + i, 0)),
            pl.BlockSpec((1, D_MODEL, CHUNK), lambda b, i: (b, 0, i)),
            pl.BlockSpec((1, D_MODEL, CHUNK), lambda b, i: (b, 0, i)),
        ],
        out_shape=[
            jax.ShapeDtypeStruct((batch * lp, D_MODEL), BF16),
            jax.ShapeDtypeStruct((batch, D_MODEL, lp), BF16),
            jax.ShapeDtypeStruct((batch, D_MODEL, lp), BF16),
        ],
        compiler_params=_params("parallel", "parallel"),
        name="kvq_proj",
    )(h, kvnw, qnw, wk, wvT, wqT)


def _rel_bucket_np(n):
    n = np.asarray(n)
    max_exact = REL_BUCKETS // 2
    nf = np.maximum(n, 1).astype(np.float32)
    large = max_exact + (np.log(nf / np.float32(max_exact)) / np.float32(math.log(REL_MAX_DIST / max_exact))
                         * np.float32(REL_BUCKETS - max_exact)).astype(np.int32)
    large = np.minimum(large, REL_BUCKETS - 1)
    return np.where(n < max_exact, n, large).astype(np.int32)


def _bucket_tiles():
    i = np.arange(TK)[:, None]
    jd = np.arange(TK)[None, :]
    j = np.arange(BAND)[None, :]
    diag = np.where(jd - i >= 0, _rel_bucket_np(np.maximum(jd - i, 0)), -1)
    a = np.arange(BAND)[:, None]
    b = np.arange(BAND)[None, :]
    corner = _rel_bucket_np(b - a + BAND)
    m = np.arange(META_KEYS)[:, None] - (META_KEYS - N_META)
    dist0 = N_META + j - m
    meta0 = np.where(m >= 0, _rel_bucket_np(np.maximum(dist0, 0)), -1)
    meta_far = np.where(m >= 0, REL_BUCKETS - 1, -1) + 0 * j
    return (corner.astype(np.int32), diag.astype(np.int32),
            np.stack([meta0, meta_far]).astype(np.int32))


def _bias_kernel(tab_ref, corner_id_ref, diag_id_ref, meta_id_ref, corner_ref, diag_ref, meta_ref):
    h = pl.program_id(0)
    far = tab_ref[REL_BUCKETS - 1, h]

    def build(ids):
        out = jnp.where(ids < 0, NEG_INF, 0.0).astype(F32)
        for b in range(REL_BUCKETS - 1):
            out = jnp.where(ids == b, (tab_ref[b, h] - far) * LOG2E, out)
        return out

    corner_ref[0] = build(corner_id_ref[...])
    diag_ref[0] = build(diag_id_ref[...])
    meta_ref[0] = build(meta_id_ref[...])


def _bias_tiles(rel_bias):
    corner_ids, diag_ids, meta_ids = _bucket_tiles()
    return pl.pallas_call(
        _bias_kernel,
        grid=(DIFF_HEADS,),
        in_specs=[
            pl.BlockSpec(memory_space=pltpu.SMEM),
            _resident((BAND, BAND)),
            _resident((TK, TK)),
            _resident((2, META_KEYS, BAND)),
        ],
        out_specs=[
            pl.BlockSpec((1, BAND, BAND), lambda h: (h, 0, 0)),
            pl.BlockSpec((1, TK, TK), lambda h: (h, 0, 0)),
            pl.BlockSpec((1, 2, META_KEYS, BAND), lambda h: (h, 0, 0, 0)),
        ],
        out_shape=[
            jax.ShapeDtypeStruct((DIFF_HEADS, BAND, BAND), F32),
            jax.ShapeDtypeStruct((DIFF_HEADS, TK, TK), F32),
            jax.ShapeDtypeStruct((DIFF_HEADS, 2, META_KEYS, BAND), F32),
        ],
        compiler_params=_params("arbitrary"),
        name="rel_bias_tiles",
    )(rel_bias, jnp.asarray(corner_ids), jnp.asarray(diag_ids), jnp.asarray(meta_ids))


def _attn_kernel(qT_ref, k_ref, vT_ref, corner_ref, diag_ref, meta_ref, lam_ref, sw_ref, o_ref,
                 s_ref, smax_ref, p_ref, alpha_ref, m_ref, acc_ref, *, n_real, lambda_init):
    qi = pl.program_id(2)
    qT = qT_ref[0]
    sub = lax.broadcasted_iota(jnp.int32, (DIFF_VDIM, 1), 0)
    q_maps = (jnp.where(sub < DIFF_HEAD_DIM, qT, jnp.zeros_like(qT)),
              jnp.where(sub >= DIFF_HEAD_DIM, qT, jnp.zeros_like(qT)))
    all_groups = tuple(range(TQ // MXU_N))

    def groups_from(lane0):
        return tuple(qg for qg in all_groups if qg * MXU_N >= lane0)

    def lanes(qg):
        return slice(qg * MXU_N, (qg + 1) * MXU_N)

    def v_ext(lo, n):
        return jnp.concatenate([vT_ref[0, :, pl.ds(lo, n)],
                                jnp.ones((V_ROWS - DIFF_VDIM, n), BF16)], axis=0)

    def key_lo(step_idx):
        if isinstance(step_idx, int):
            return max(step_idx, 0) * TK
        return pl.multiple_of(jnp.maximum(step_idx, 0) * TK, TK)

    def scores_into(slot, lo, groups):
        k_blk = k_ref[pl.ds(lo, TK), :]
        for c in range(2):
            for qg in groups:
                s = jnp.dot(k_blk, q_maps[c][:, lanes(qg)], preferred_element_type=F32)
                s_ref[slot, c, :, lanes(qg)] = s
                smax_ref[slot, c, :, lanes(qg)] = jnp.max(s, axis=0, keepdims=True)

    def add_corner(slot, lane0):
        for c in range(2):
            blk = (slot, c, slice(TK - BAND, TK), slice(lane0, lane0 + BAND))
            s_ref[blk] = s_ref[blk] + corner_ref[0]
            qs = slice(lane0, lane0 + MXU_N)
            smax_ref[slot, c, :, qs] = jnp.max(s_ref[slot, c, :, qs], axis=0, keepdims=True)

    def accumulate(slot, lo, groups):
        vx = v_ext(lo, TK)
        for c in range(2):
            for qg in groups:
                qs = lanes(qg)
                acc_ref[c, :, qs] = (alpha_ref[slot, c, :, qs] * acc_ref[c, :, qs]
                                     + jnp.dot(vx, p_ref[slot, c, :, qs], preferred_element_type=F32))

    def step(slot, idx, groups=all_groups, diag_lane0=None, nxt=all_groups, lag=all_groups):
        if nxt:
            scores_into(1 - slot, key_lo(idx - 1), nxt)
        for c in range(2):
            for qg in groups:
                qs = lanes(qg)
                s = s_ref[slot, c, :, qs]
                on_diag = diag_lane0 is not None and 0 <= qg * MXU_N - diag_lane0 < TK
                if on_diag:
                    s = s + diag_ref[0, :, qg * MXU_N - diag_lane0:(qg + 1) * MXU_N - diag_lane0]
                    s_max = jnp.max(s, axis=0, keepdims=True)
                else:
                    s_max = smax_ref[slot, c, :, qs]
                m_old = m_ref[c, :, qs]
                m_new = jnp.maximum(m_old, s_max)
                alpha_ref[slot, c, :, qs] = jnp.exp2(m_old - m_new)
                p_ref[slot, c, :, qs] = jnp.exp2(s - m_new).astype(BF16)
                m_ref[c, :, qs] = m_new
        if lag:
            accumulate(1 - slot, key_lo(idx + 1), lag)

    meta_lo = n_real + CHUNK - META_KEYS
    k_m = k_ref[meta_lo:meta_lo + META_KEYS, :]
    vx_m = v_ext(meta_lo, META_KEYS)
    band_m = meta_ref[0, jnp.minimum(qi, 1)]
    mask_m = meta_ref[0, 1, :, 0:1]
    for c in range(2):
        s = jnp.dot(k_m, q_maps[c], preferred_element_type=F32)
        s = jnp.concatenate([s[:, :BAND] + band_m, s[:, BAND:] + mask_m], axis=1)
        m_c = jnp.max(s, axis=0, keepdims=True)
        m_ref[c] = m_c
        acc_ref[c] = jnp.dot(vx_m, jnp.exp2(s - m_c).astype(BF16), preferred_element_type=F32)

    first = KEY_STEPS * qi
    scores_into(0, key_lo(first + KEY_STEPS - 1), groups_from((KEY_STEPS - 1) * TK))
    for d in range(KEY_STEPS - 1, -1, -1):
        slot = (KEY_STEPS - 1 - d) % 2
        if d < KEY_STEPS - 1:
            add_corner(slot, (d + 1) * TK)
        step(slot, first + d, groups_from(d * TK), diag_lane0=d * TK,
             nxt=groups_from((d - 1) * TK) if d > 0 else all_groups,
             lag=groups_from((d + 1) * TK) if d < KEY_STEPS - 1 else ())

    @pl.when(qi > 0)
    def _():
        add_corner(0, 0)

        def pair(i, carry):
            idx = first - 1 - 2 * i
            step(0, idx)
            step(1, idx - 1)
            return carry

        lax.fori_loop(0, first // 2 - 1, pair, 0)
        step(0, 1)
        step(1, 0, nxt=())

    accumulate(1, key_lo(0), all_groups)

    lv = lam_ref[...]
    lam = (jnp.exp(jnp.sum(lv[0:1] * lv[1:2], axis=-1, keepdims=True))
           - jnp.exp(jnp.sum(lv[2:3] * lv[3:4], axis=-1, keepdims=True)) + lambda_init)
    o = [acc_ref[c, 0:DIFF_VDIM, :] / acc_ref[c, DIFF_VDIM:DIFF_VDIM + 1, :] for c in range(2)]
    o = o[0] - lam * o[1]
    o = o * lax.rsqrt(jnp.mean(o * o, axis=0, keepdims=True) + EPS)
    o = o * sw_ref[...] * (1.0 - lambda_init)
    o_ref[...] = o.T.astype(BF16)


def _attention(qT, k, vT, corner, diag, meta, lamv, sw, *, batch, n_real, lp, lambda_init):
    nq = n_real // TQ
    return pl.pallas_call(
        functools.partial(_attn_kernel, n_real=n_real, lambda_init=lambda_init),
        grid=(batch, DIFF_HEADS, nq),
        in_specs=[
            pl.BlockSpec((1, DIFF_VDIM, TQ), lambda b, h, i: (b, h, i)),
            pl.BlockSpec((lp, DIFF_VDIM), lambda b, h, i: (b, h)),
            pl.BlockSpec((1, DIFF_VDIM, lp), lambda b, h, i: (b, h, 0)),
            pl.BlockSpec((1, BAND, BAND), lambda b, h, i: (h, 0, 0)),
            pl.BlockSpec((1, TK, TK), lambda b, h, i: (h, 0, 0)),
            pl.BlockSpec((1, 2, META_KEYS, BAND), lambda b, h, i: (h, 0, 0, 0)),
            _resident((4, DIFF_HEAD_DIM)),
            _resident((DIFF_VDIM, 1)),
        ],
        out_specs=pl.BlockSpec((TQ, DIFF_VDIM), lambda b, h, i: (b * nq + i, h)),
        out_shape=jax.ShapeDtypeStruct((batch * n_real, D_MODEL), BF16),
        scratch_shapes=[
            pltpu.VMEM((2, 2, TK, TQ), F32),
            pltpu.VMEM((2, 2, 1, TQ), F32),
            pltpu.VMEM((2, 2, TK, TQ), BF16),
            pltpu.VMEM((2, 2, 1, TQ), F32),
            pltpu.VMEM((2, 1, TQ), F32),
            pltpu.VMEM((2, V_ROWS, TQ), F32),
        ],
        compiler_params=_params("parallel", "parallel", "arbitrary"),
        name="diff_attention",
    )(qT, k, vT, corner, diag, meta, lamv, sw)


def kernel(x, meta_tokens, norm_w, ssm_in_w, ssm_conv_w, ssm_conv_b, ssm_dt_bias, ssm_a_log, ssm_d,
           ssm_norm_w, ssm_out_w, kv_norm_w, w_kv, w_q, lam_q1, lam_k1, lam_q2, lam_k2, subln_w,
           w_o, rel_bias, ffn_w_gu, ffn_w_down, final_norm_w):
    batch, n_real, d = x.shape
    assert d == D_MODEL and n_real % TQ == 0 and n_real % ROW_TILE == 0 and KEY_STEPS % 2 == 0
    assert norm_w.shape[0] == 2 and ssm_in_w.shape[0] == 1 and w_q.shape[0] == 1
    lp = n_real + CHUNK
    n_pad = CHUNK - N_META
    n_chunks = lp // CHUNK
    rows = batch * lp
    assert rows % ROW_TILE == 0
    row2 = lambda v: v.reshape(1, -1).astype(F32)

    tail = jnp.concatenate([jnp.zeros((n_pad, D_MODEL), F32), meta_tokens.astype(F32)], axis=0)

    in_w = ssm_in_w[0]
    w_zx = in_w[:, :D_INNER + CONV_DIM].astype(BF16)
    w_dt = jnp.pad(in_w[:, D_INNER + CONV_DIM:].astype(F32), ((0, 0), (0, LANES - SSM_HEADS)))
    w_dt_hi = w_dt.astype(BF16)
    w_dt_lo = (w_dt - w_dt_hi.astype(F32)).astype(BF16)
    z, xc, dt, dtT = _in_proj(x, tail, row2(norm_w[0, 0]), w_zx, w_dt_hi, w_dt_lo, ssm_conv_w[0].astype(F32),
                              row2(ssm_conv_b[0]), batch=batch, n_chunks=n_chunks)
    y = _ssd(xc, dt, dtT, row2(ssm_dt_bias[0]), ssm_dt_bias[0].reshape(-1, 1),
             row2(ssm_a_log[0]), ssm_a_log[0].reshape(-1, 1),
             row2(jnp.repeat(ssm_d[0], SSM_HEADDIM)), batch=batch, n_chunks=n_chunks, n_pad=n_pad)
    h2 = _ssd_out_ffn(y, z, row2(ssm_norm_w[0]), ssm_out_w[0].astype(BF16), x, tail, row2(norm_w[0, 1]),
                      ffn_w_gu[0].astype(BF16), ffn_w_down[0].astype(BF16), batch=batch, n_chunks=n_chunks)

    layer = 1
    lambda_init = 0.8 - 0.6 * math.exp(-0.3 * layer)
    k, vT, qT = _kvq(h2, row2(kv_norm_w), row2(norm_w[1, 0]), w_kv[:, :D_MODEL].astype(BF16),
                     w_kv[:, D_MODEL:].T.astype(BF16), w_q[0].T.astype(BF16), batch=batch, lp=lp)
    corner, diag, meta = _bias_tiles(rel_bias.astype(F32))
    lamv = jnp.stack([lam_q1[0], lam_k1[0], lam_q2[0], lam_k2[0]]).astype(F32)
    attn = _attention(qT, k, vT, corner, diag, meta, lamv, subln_w[0].reshape(-1, 1).astype(F32),
                      batch=batch, n_real=n_real, lp=lp, lambda_init=lambda_init)
    out = _attn_out_ffn(attn, w_o[0].astype(BF16), h2.reshape(batch, lp, D_MODEL), row2(norm_w[1, 1]),
                        ffn_w_gu[1].astype(BF16), ffn_w_down[1].astype(BF16),
                        row2(final_norm_w), batch=batch, n_real=n_real)
    return out.reshape(batch, n_real, D_MODEL)
```

```python
import functools
import math

import numpy as np
import jax
import jax.numpy as jnp
from jax import lax
from jax.experimental import pallas as pl
from jax.experimental.pallas import tpu as pltpu

F32 = jnp.float32
BF16 = jnp.bfloat16

D_MODEL = 1024
N_META = 16
EPS = 1e-6
NEG_INF = -1e30
SSM_HEADDIM = 64
SSM_HEADS = 32
SSM_GROUPS = 8
SSM_HPG = SSM_HEADS // SSM_GROUPS
SSM_STATE = 128
SSM_CONV = 4
D_INNER = SSM_HEADS * SSM_HEADDIM
GROUP_W = D_INNER // SSM_GROUPS
CONV_DIM = D_INNER + 2 * SSM_GROUPS * SSM_STATE
DIFF_HEADS = 8
DIFF_HEAD_DIM = 64
DIFF_VDIM = 2 * DIFF_HEAD_DIM
REL_BUCKETS = 32
REL_MAX_DIST = 128
FFN_HIDDEN = 2816

SUBLANES = 8
LANES = 128
CHUNK = 256
TIME_BLOCKS = CHUNK // SUBLANES
ROW_TILE = 512
CONV_TILE = 512
MXU_N = 256
TK = 512
KEY_STEPS = 4
TQ = KEY_STEPS * TK
BAND = REL_MAX_DIST
META_KEYS = 128
V_ROWS = DIFF_VDIM + 16
LOG2E = math.log2(math.e)
VMEM_LIMIT = 56 * 1024 * 1024


def _resident(shape):
    nd = len(shape)
    return pl.BlockSpec(shape, lambda *_: (0,) * nd, pipeline_mode=pl.Buffered(1))


def _params(*sem, fuse=None):
    return pltpu.CompilerParams(dimension_semantics=sem, vmem_limit_bytes=VMEM_LIMIT,
                                allow_input_fusion=fuse)


def _rmsnorm(x, w):
    return x * lax.rsqrt(jnp.mean(x * x, axis=-1, keepdims=True) + EPS) * w


def _softplus(v):
    return jnp.maximum(v, 0.0) + jnp.log(1.0 + jnp.exp(-jnp.abs(v)))


def _silu(v):
    return v * jax.nn.sigmoid(v)


def _split3(v):
    hi = v.astype(BF16)
    r1 = v - hi.astype(F32)
    mid = r1.astype(BF16)
    lo = (r1 - mid.astype(F32)).astype(BF16)
    return hi, mid, lo


_NT = (((1,), (1,)), ((), ()))


def _chunk_time(r):
    return jnp.bitwise_and(r, SUBLANES - 1) * TIME_BLOCKS + jnp.right_shift(r, SUBLANES.bit_length() - 1)


def _in_proj_kernel(x_ref, tail_ref, nw_ref, w_ref, wdh_ref, wdl_ref, cw_ref, cb_ref,
                    z_ref, xc_ref, dt_ref, dtT_ref, xe_ref, carry_ref):
    Q = CHUNK
    halo = (SSM_CONV - 1) * SUBLANES
    c = pl.program_id(1)

    @pl.when(c == 0)
    def _():
        carry_ref[...] = jnp.zeros_like(carry_ref)

    xn = _rmsnorm(jnp.where(c == 0, tail_ref[...], x_ref[0]), nw_ref[...])
    xb = xn.astype(BF16)
    x_lo = (xn - xb.astype(F32)).astype(BF16)
    ii = lax.broadcasted_iota(jnp.int32, (Q, Q), 0)
    jj = lax.broadcasted_iota(jnp.int32, (Q, Q), 1)
    perm = (_chunk_time(ii) == jj).astype(BF16)
    xb_t = jnp.dot(perm, xb, preferred_element_type=F32).astype(BF16)
    x_lo_t = jnp.dot(perm, x_lo, preferred_element_type=F32).astype(BF16)
    first_sub = lax.broadcasted_iota(jnp.int32, (SUBLANES, 1), 0) == 0
    z_tile = CONV_TILE * D_INNER // CONV_DIM
    for t in range(CONV_DIM // CONV_TILE):
        c0 = t * CONV_TILE
        cs = slice(c0, c0 + CONV_TILE)
        pre = jnp.dot(xb_t, w_ref[:, D_INNER + c0:D_INNER + c0 + CONV_TILE], preferred_element_type=F32)
        xe_ref[halo:halo + Q, cs] = pre
        for j in range(SSM_CONV - 1):
            blk = slice(j * SUBLANES, (j + 1) * SUBLANES)
            cur = pre[Q - halo + j * SUBLANES:Q - halo + (j + 1) * SUBLANES, :]
            xe_ref[blk, cs] = jnp.where(first_sub, pltpu.roll(carry_ref[blk, cs], 1, 0),
                                        pltpu.roll(cur, 1, 0))
            carry_ref[blk, cs] = cur
        acc = cb_ref[:, cs] + cw_ref[SSM_CONV - 1:SSM_CONV, cs] * pre
        for k in range(SSM_CONV - 1):
            off = halo - (SSM_CONV - 1 - k) * SUBLANES
            acc = acc + cw_ref[k:k + 1, cs] * xe_ref[off:off + Q, cs]
        xc_ref[:, cs] = _silu(acc).astype(BF16)
        zs = slice(t * z_tile, (t + 1) * z_tile)
        z_ref[:, zs] = jnp.dot(xb, w_ref[:, zs], preferred_element_type=F32).astype(BF16)
    dt = (jnp.dot(xb_t, wdh_ref[...], preferred_element_type=F32)
          + jnp.dot(x_lo_t, wdh_ref[...], preferred_element_type=F32)
          + jnp.dot(xb_t, wdl_ref[...], preferred_element_type=F32))
    dt_ref[...] = dt[:, 0:SSM_HEADS]
    dtT_ref[...] = dt.T[0:SSM_HEADS, :]


def _scan_block(n_chunks):
    return lambda b, c: b * n_chunks + (c + n_chunks - 1) % n_chunks


def _in_proj(x3, tail, nw, w_zx, w_dt_hi, w_dt_lo, cw, cb, *, batch, n_chunks):
    rows = batch * n_chunks * CHUNK
    blk = _scan_block(n_chunks)
    return pl.pallas_call(
        _in_proj_kernel,
        grid=(batch, n_chunks),
        in_specs=[
            pl.BlockSpec((1, CHUNK, D_MODEL), lambda b, c: (b, jnp.maximum(c - 1, 0), 0)),
            _resident((CHUNK, D_MODEL)),
            _resident((1, D_MODEL)),
            _resident((D_MODEL, D_INNER + CONV_DIM)),
            _resident((D_MODEL, LANES)),
            _resident((D_MODEL, LANES)),
            _resident((SSM_CONV, CONV_DIM)),
            _resident((1, CONV_DIM)),
        ],
        out_specs=[
            pl.BlockSpec((CHUNK, D_INNER), lambda b, c: (blk(b, c), 0)),
            pl.BlockSpec((CHUNK, CONV_DIM), lambda b, c: (blk(b, c), 0)),
            pl.BlockSpec((CHUNK, SSM_HEADS), lambda b, c: (blk(b, c), 0)),
            pl.BlockSpec((SSM_HEADS, CHUNK), lambda b, c: (0, blk(b, c))),
        ],
        out_shape=[
            jax.ShapeDtypeStruct((rows, D_INNER), BF16),
            jax.ShapeDtypeStruct((rows, CONV_DIM), BF16),
            jax.ShapeDtypeStruct((rows, SSM_HEADS), F32),
            jax.ShapeDtypeStruct((SSM_HEADS, rows), F32),
        ],
        scratch_shapes=[
            pltpu.VMEM((CHUNK + (SSM_CONV - 1) * SUBLANES, CONV_DIM), F32),
            pltpu.VMEM(((SSM_CONV - 1) * SUBLANES, CONV_DIM), F32),
        ],
        compiler_params=_params("arbitrary", "arbitrary",
                                fuse=[False, False, False, True, True, True, False, False]),
        name="in_proj",
    )(x3, tail, nw, w_zx, w_dt_hi, w_dt_lo, cw, cb)


def _ssd_kernel(xc_ref, dt_ref, dtT_ref, dtb_ref, dtbT_ref, alog_ref, alogT_ref,
                dsk_ref, y_ref, state_ref, causal_ref, tri_ref, *, n_pad):
    Q = CHUNK
    c = pl.program_id(1)

    @pl.when(c == 0)
    def _():
        state_ref[...] = jnp.zeros_like(state_ref)
        ti = _chunk_time(lax.broadcasted_iota(jnp.int32, (Q, Q), 0))
        tj = _chunk_time(lax.broadcasted_iota(jnp.int32, (Q, Q), 1))
        causal_ref[...] = jnp.where(ti >= tj, 0.0, NEG_INF)
        tri_ref[0] = (ti >= tj).astype(BF16)
        tri_ref[1] = (ti <= tj).astype(BF16)

    first_valid = jnp.where(c == 0, n_pad, 0)
    t_col = _chunk_time(lax.broadcasted_iota(jnp.int32, (Q, 1), 0))
    t_row = _chunk_time(lax.broadcasted_iota(jnp.int32, (1, Q), 1))
    dtc = jnp.where(t_col >= first_valid, _softplus(dt_ref[...] + dtb_ref[...]), 0.0)
    dtr = jnp.where(t_row >= first_valid, _softplus(dtT_ref[...] + dtbT_ref[...]), 0.0)
    dac = dtc * (-jnp.exp(alog_ref[...]))
    dar = dtr * (-jnp.exp(alogT_ref[...]))
    ac2 = sum(jnp.dot(tri_ref[0], t, preferred_element_type=F32) for t in _split3(dac)) * LOG2E
    ar2 = sum(jnp.dot(t, tri_ref[1], preferred_element_type=F32) for t in _split3(dar)) * LOG2E
    ar2_dt = ar2 - jnp.where(dtr > 0.0, jnp.log2(dtr), NEG_INF)
    a_last2 = ar2[:, Q - 1:Q]
    w_state = jnp.exp2(a_last2 - ar2_dt)
    e_last = jnp.exp2(a_last2)
    eacs_c = jnp.exp2(ac2)
    lane = lax.broadcasted_iota(jnp.int32, (1, GROUP_W), 1)

    for g in range(SSM_GROUPS):
        b_g = xc_ref[:, D_INNER + g * SSM_STATE:D_INNER + (g + 1) * SSM_STATE]
        c_g = xc_ref[:, D_INNER + (SSM_GROUPS + g) * SSM_STATE:D_INNER + (SSM_GROUPS + g + 1) * SSM_STATE]
        x_g = xc_ref[:, g * GROUP_W:(g + 1) * GROUP_W]
        cb = lax.dot_general(c_g, b_g, _NT, preferred_element_type=F32)
        b_t = b_g.astype(F32).T
        s_old = state_ref[g]
        y_off = jnp.dot(c_g, s_old.astype(BF16), preferred_element_type=F32)
        y_diag = jnp.zeros((Q, GROUP_W), F32)
        s_add = jnp.zeros((SSM_STATE, GROUP_W), F32)
        scale = jnp.zeros((Q, GROUP_W), F32)
        sdec = jnp.zeros((1, GROUP_W), F32)
        for r in range(SSM_HPG):
            h = g * SSM_HPG + r
            decay_dt = jnp.exp2(ac2[:, h:h + 1] - ar2_dt[h:h + 1, :] + causal_ref[...])
            w = (cb * decay_dt).astype(BF16)
            in_head = (lane >= r * SSM_HEADDIM) & (lane < (r + 1) * SSM_HEADDIM)
            x_r = jnp.where(in_head, x_g, jnp.zeros_like(x_g))
            y_diag = y_diag + jnp.dot(w, x_r, preferred_element_type=F32)
            s_add = s_add + jnp.dot((b_t * w_state[h:h + 1, :]).astype(BF16), x_r,
                                    preferred_element_type=F32)
            scale = jnp.where(in_head, eacs_c[:, h:h + 1], scale)
            sdec = jnp.where(in_head, e_last[h:h + 1, :], sdec)
        state_ref[g] = s_old * sdec + s_add
        gs = slice(g * GROUP_W, (g + 1) * GROUP_W)
        y_ref[:, gs] = (y_diag + y_off * scale + x_g.astype(F32) * dsk_ref[:, gs]).astype(BF16)


def _ssd(xc, dt, dtT, dtb, dtbT, alog, alogT, dsk, *, batch, n_chunks, n_pad):
    rows = xc.shape[0]
    blk = _scan_block(n_chunks)
    return pl.pallas_call(
        functools.partial(_ssd_kernel, n_pad=n_pad),
        grid=(batch, n_chunks),
        in_specs=[
            pl.BlockSpec((CHUNK, CONV_DIM), lambda b, c: (blk(b, c), 0)),
            pl.BlockSpec((CHUNK, SSM_HEADS), lambda b, c: (blk(b, c), 0)),
            pl.BlockSpec((SSM_HEADS, CHUNK), lambda b, c: (0, blk(b, c))),
            _resident((1, SSM_HEADS)),
            _resident((SSM_HEADS, 1)),
            _resident((1, SSM_HEADS)),
            _resident((SSM_HEADS, 1)),
            _resident((1, D_INNER)),
        ],
        out_specs=pl.BlockSpec((CHUNK, D_INNER), lambda b, c: (blk(b, c), 0)),
        out_shape=jax.ShapeDtypeStruct((rows, D_INNER), BF16),
        scratch_shapes=[
            pltpu.VMEM((SSM_GROUPS, SSM_STATE, GROUP_W), F32),
            pltpu.VMEM((CHUNK, CHUNK), F32),
            pltpu.VMEM((2, CHUNK, CHUNK), BF16),
        ],
        compiler_params=_params("arbitrary", "arbitrary"),
        name="ssd_scan",
    )(xc, dt, dtT, dtb, dtbT, alog, alogT, dsk)


def _swiglu_residual(x, nw_ref, wgu_ref, wd_ref):
    xb = _rmsnorm(x, nw_ref[...]).astype(BF16)
    gate_val = jnp.dot(xb, wgu_ref[...], preferred_element_type=F32)
    act = (_silu(gate_val[:, :FFN_HIDDEN]) * gate_val[:, FFN_HIDDEN:]).astype(BF16)
    return x + jnp.dot(act, wd_ref[...], preferred_element_type=F32)


def _ssd_out_ffn_kernel(y_ref, z_ref, gnw_ref, wo_ref, x_ref, tail_ref, nw_ref, wgu_ref, wd_ref, o_ref,
                        *, n_chunks):
    ii = lax.broadcasted_iota(jnp.int32, (CHUNK, CHUNK), 0)
    jj = lax.broadcasted_iota(jnp.int32, (CHUNK, CHUNK), 1)
    unperm = (ii == _chunk_time(jj)).astype(BF16)
    parts = []
    for g in range(SSM_GROUPS):
        gs = slice(g * GROUP_W, (g + 1) * GROUP_W)
        y = jnp.dot(unperm, y_ref[:, gs], preferred_element_type=F32)
        y = y * _silu(z_ref[:, gs].astype(F32))
        parts.append(_rmsnorm(y, gnw_ref[:, gs]).astype(BF16))
    a = jnp.concatenate(parts, axis=-1)
    res = jnp.where(pl.program_id(1) == n_chunks - 1, tail_ref[...], x_ref[0])
    h = res + jnp.dot(a, wo_ref[...], preferred_element_type=F32)
    o_ref[...] = _swiglu_residual(h, nw_ref, wgu_ref, wd_ref)


def _ssd_out_ffn(y, z, gnw, wo, x3, tail, nw, wgu, wd, *, batch, n_chunks):
    rows = y.shape[0]
    tile = lambda w: pl.BlockSpec((CHUNK, w), lambda b, j: (b * n_chunks + j, 0))
    return pl.pallas_call(
        functools.partial(_ssd_out_ffn_kernel, n_chunks=n_chunks),
        grid=(batch, n_chunks),
        in_specs=[
            tile(D_INNER), tile(D_INNER), _resident((1, D_INNER)), _resident((D_INNER, D_MODEL)),
            pl.BlockSpec((1, CHUNK, D_MODEL), lambda b, j: (b, jnp.minimum(j, n_chunks - 2), 0)),
            _resident((CHUNK, D_MODEL)),
            _resident((1, D_MODEL)), _resident((D_MODEL, 2 * FFN_HIDDEN)),
            _resident((FFN_HIDDEN, D_MODEL)),
        ],
        out_specs=tile(D_MODEL),
        out_shape=jax.ShapeDtypeStruct((rows, D_MODEL), F32),
        compiler_params=_params("parallel", "parallel",
                                fuse=[False, False, False, True, False, False, False, True, True]),
        name="ssd_out_ffn",
    )(y, z, gnw, wo, x3, tail, nw, wgu, wd)


def _attn_out_ffn_kernel(a_ref, wo_ref, r_ref, nw_ref, wgu_ref, wd_ref, fnw_ref, o_ref):
    h = r_ref[0] + jnp.dot(a_ref[...], wo_ref[...], preferred_element_type=F32)
    o_ref[...] = _rmsnorm(_swiglu_residual(h, nw_ref, wgu_ref, wd_ref), fnw_ref[...])


def _attn_out_ffn(a, wo, res3, nw, wgu, wd, fnw, *, batch, n_real):
    tiles = n_real // ROW_TILE
    return pl.pallas_call(
        _attn_out_ffn_kernel,
        grid=(batch, tiles),
        in_specs=[
            pl.BlockSpec((ROW_TILE, D_MODEL), lambda b, i: (b * tiles + i, 0)),
            _resident((D_MODEL, D_MODEL)),
            pl.BlockSpec((1, ROW_TILE, D_MODEL), lambda b, i: (b, i, 0)),
            _resident((1, D_MODEL)), _resident((D_MODEL, 2 * FFN_HIDDEN)),
            _resident((FFN_HIDDEN, D_MODEL)), _resident((1, D_MODEL)),
        ],
        out_specs=pl.BlockSpec((ROW_TILE, D_MODEL), lambda b, i: (b * tiles + i, 0)),
        out_shape=jax.ShapeDtypeStruct((batch * n_real, D_MODEL), F32),
        compiler_params=_params("parallel", "parallel",
                                fuse=[False, True, False, False, True, True, False]),
        name="attn_out_ffn",
    )(a, wo, res3, nw, wgu, wd, fnw)


def _kvq_kernel(h_ref, kvnw_ref, qnw_ref, wk_ref, wvT_ref, wqT_ref, k_ref, vT_ref, qT_ref):
    x = h_ref[...]
    inv = lax.rsqrt(jnp.mean(x * x, axis=-1, keepdims=True) + EPS)
    xkv = (x * inv * kvnw_ref[...]).astype(BF16)
    xq = (x * inv * qnw_ref[...]).astype(BF16)
    k_ref[...] = jnp.dot(xkv, wk_ref[...], preferred_element_type=F32).astype(BF16)
    vT_ref[0] = lax.dot_general(wvT_ref[...], xkv, _NT, preferred_element_type=F32).astype(BF16)
    qT = lax.dot_general(wqT_ref[...], xq, _NT, preferred_element_type=F32)
    qT_ref[0] = (qT * (DIFF_HEAD_DIM ** -0.5 * LOG2E)).astype(BF16)


def _kvq(h, kvnw, qnw, wk, wvT, wqT, *, batch, lp):
    tiles = lp // CHUNK
    return pl.pallas_call(
        _kvq_kernel,
        grid=(batch, tiles),
        in_specs=[
            pl.BlockSpec((CHUNK, D_MODEL), lambda b, i: (b * tiles + i, 0)),
            _resident((1, D_MODEL)),
            _resident((1, D_MODEL)),
            _resident((D_MODEL, D_MODEL)),
            _resident((D_MODEL, D_MODEL)),
            _resident((D_MODEL, D_MODEL)),
        ],
        out_specs=[
            pl.BlockSpec((CHUNK, D_MODEL), lambda b, i: (b * tiles + i, 0)),
            pl.BlockSpec((1, D_MODEL, CHUNK), lambda b, i: (b, 0, i)),
            pl.BlockSpec((1, D_MODEL, CHUNK), lambda b, i: (b, 0, i)),
        ],
        out_shape=[
            jax.ShapeDtypeStruct((batch * lp, D_MODEL), BF16),
            jax.ShapeDtypeStruct((batch, D_MODEL, lp), BF16),
            jax.ShapeDtypeStruct((batch, D_MODEL, lp), BF16),
        ],
        compiler_params=_params("parallel", "parallel",
                                fuse=[False, False, False, True, True, True]),
        name="kvq_proj",
    )(h, kvnw, qnw, wk, wvT, wqT)


def _rel_bucket_np(n):
    n = np.asarray(n)
    max_exact = REL_BUCKETS // 2
    nf = np.maximum(n, 1).astype(np.float32)
    large = max_exact + (np.log(nf / np.float32(max_exact)) / np.float32(math.log(REL_MAX_DIST / max_exact))
                         * np.float32(REL_BUCKETS - max_exact)).astype(np.int32)
    large = np.minimum(large, REL_BUCKETS - 1)
    return np.where(n < max_exact, n, large).astype(np.int32)


def _bucket_tiles():
    i = np.arange(TK)[:, None]
    jd = np.arange(TK)[None, :]
    j = np.arange(BAND)[None, :]
    diag = np.where(jd - i >= 0, _rel_bucket_np(np.maximum(jd - i, 0)), -1)
    a = np.arange(BAND)[:, None]
    b = np.arange(BAND)[None, :]
    corner = _rel_bucket_np(b - a + BAND)
    m = np.arange(META_KEYS)[:, None] - (META_KEYS - N_META)
    dist0 = N_META + j - m
    meta0 = np.where(m >= 0, _rel_bucket_np(np.maximum(dist0, 0)), -1)
    meta_far = np.where(m >= 0, REL_BUCKETS - 1, -1) + 0 * j
    return (corner.astype(np.int32), diag.astype(np.int32),
            np.stack([meta0, meta_far]).astype(np.int32))


def _bias_kernel(tab_ref, corner_id_ref, diag_id_ref, meta_id_ref, corner_ref, diag_ref, meta_ref):
    h = pl.program_id(0)
    far = tab_ref[REL_BUCKETS - 1, h]

    def build(ids):
        out = jnp.where(ids < 0, NEG_INF, 0.0).astype(F32)
        for b in range(REL_BUCKETS - 1):
            out = jnp.where(ids == b, (tab_ref[b, h] - far) * LOG2E, out)
        return out

    corner_ref[0] = build(corner_id_ref[...])
    diag_ref[0] = build(diag_id_ref[...])
    meta_ref[0] = build(meta_id_ref[...])


def _bias_tiles(rel_bias):
    corner_ids, diag_ids, meta_ids = _bucket_tiles()
    return pl.pallas_call(
        _bias_kernel,
        grid=(DIFF_HEADS,),
        in_specs=[
            pl.BlockSpec(memory_space=pltpu.SMEM),
            _resident((BAND, BAND)),
            _resident((TK, TK)),
            _resident((2, META_KEYS, BAND)),
        ],
        out_specs=[
            pl.BlockSpec((1, BAND, BAND), lambda h: (h, 0, 0)),
            pl.BlockSpec((1, TK, TK), lambda h: (h, 0, 0)),
            pl.BlockSpec((1, 2, META_KEYS, BAND), lambda h: (h, 0, 0, 0)),
        ],
        out_shape=[
            jax.ShapeDtypeStruct((DIFF_HEADS, BAND, BAND), F32),
            jax.ShapeDtypeStruct((DIFF_HEADS, TK, TK), F32),
            jax.ShapeDtypeStruct((DIFF_HEADS, 2, META_KEYS, BAND), F32),
        ],
        compiler_params=_params("arbitrary"),
        name="rel_bias_tiles",
    )(rel_bias, jnp.asarray(corner_ids), jnp.asarray(diag_ids), jnp.asarray(meta_ids))


def _attn_kernel(qT_ref, k_ref, vT_ref, corner_ref, diag_ref, meta_ref, lam_ref, sw_ref, o_ref,
                 s_ref, smax_ref, p_ref, alpha_ref, m_ref, acc_ref, *, n_real, lambda_init):
    qi = pl.program_id(2)
    qT = qT_ref[0]
    sub = lax.broadcasted_iota(jnp.int32, (DIFF_VDIM, 1), 0)
    q_maps = (jnp.where(sub < DIFF_HEAD_DIM, qT, jnp.zeros_like(qT)),
              jnp.where(sub >= DIFF_HEAD_DIM, qT, jnp.zeros_like(qT)))
    all_groups = tuple(range(TQ // MXU_N))

    def groups_from(lane0):
        return tuple(qg for qg in all_groups if qg * MXU_N >= lane0)

    def lanes(qg):
        return slice(qg * MXU_N, (qg + 1) * MXU_N)

    def v_ext(lo, n):
        return jnp.concatenate([vT_ref[0, :, pl.ds(lo, n)],
                                jnp.ones((V_ROWS - DIFF_VDIM, n), BF16)], axis=0)

    def key_lo(step_idx):
        if isinstance(step_idx, int):
            return max(step_idx, 0) * TK
        return pl.multiple_of(jnp.maximum(step_idx, 0) * TK, TK)

    def scores_into(slot, lo, groups):
        k_blk = k_ref[pl.ds(lo, TK), :]
        for c in range(2):
            for qg in groups:
                s = jnp.dot(k_blk, q_maps[c][:, lanes(qg)], preferred_element_type=F32)
                s_ref[slot, c, :, lanes(qg)] = s
                smax_ref[slot, c, :, lanes(qg)] = jnp.max(s, axis=0, keepdims=True)

    def add_corner(slot, lane0):
        for c in range(2):
            blk = (slot, c, slice(TK - BAND, TK), slice(lane0, lane0 + BAND))
            s_ref[blk] = s_ref[blk] + corner_ref[0]
            qs = slice(lane0, lane0 + MXU_N)
            smax_ref[slot, c, :, qs] = jnp.max(s_ref[slot, c, :, qs], axis=0, keepdims=True)

    def accumulate(slot, lo, groups):
        vx = v_ext(lo, TK)
        for c in range(2):
            for qg in groups:
                qs = lanes(qg)
                acc_ref[c, :, qs] = (alpha_ref[slot, c, :, qs] * acc_ref[c, :, qs]
                                     + jnp.dot(vx, p_ref[slot, c, :, qs], preferred_element_type=F32))

    def step(slot, idx, groups=all_groups, diag_lane0=None, nxt=all_groups, lag=all_groups):
        if nxt:
            scores_into(1 - slot, key_lo(idx - 1), nxt)
        for c in range(2):
            for qg in groups:
                qs = lanes(qg)
                s = s_ref[slot, c, :, qs]
                on_diag = diag_lane0 is not None and 0 <= qg * MXU_N - diag_lane0 < TK
                if on_diag:
                    s = s + diag_ref[0, :, qg * MXU_N - diag_lane0:(qg + 1) * MXU_N - diag_lane0]
                    s_max = jnp.max(s, axis=0, keepdims=True)
                else:
                    s_max = smax_ref[slot, c, :, qs]
                m_old = m_ref[c, :, qs]
                m_new = jnp.maximum(m_old, s_max)
                alpha_ref[slot, c, :, qs] = jnp.exp2(m_old - m_new)
                p_ref[slot, c, :, qs] = jnp.exp2(s - m_new).astype(BF16)
                m_ref[c, :, qs] = m_new
        if lag:
            accumulate(1 - slot, key_lo(idx + 1), lag)

    meta_lo = n_real + CHUNK - META_KEYS
    k_m = k_ref[meta_lo:meta_lo + META_KEYS, :]
    vx_m = v_ext(meta_lo, META_KEYS)
    band_m = meta_ref[0, jnp.minimum(qi, 1)]
    mask_m = meta_ref[0, 1, :, 0:1]
    for c in range(2):
        s = jnp.dot(k_m, q_maps[c], preferred_element_type=F32)
        s = jnp.concatenate([s[:, :BAND] + band_m, s[:, BAND:] + mask_m], axis=1)
        m_c = jnp.max(s, axis=0, keepdims=True)
        m_ref[c] = m_c
        acc_ref[c] = jnp.dot(vx_m, jnp.exp2(s - m_c).astype(BF16), preferred_element_type=F32)

    first = KEY_STEPS * qi
    scores_into(0, key_lo(first + KEY_STEPS - 1), groups_from((KEY_STEPS - 1) * TK))
    for d in range(KEY_STEPS - 1, -1, -1):
        slot = (KEY_STEPS - 1 - d) % 2
        if d < KEY_STEPS - 1:
            add_corner(slot, (d + 1) * TK)
        step(slot, first + d, groups_from(d * TK), diag_lane0=d * TK,
             nxt=groups_from((d - 1) * TK) if d > 0 else all_groups,
             lag=groups_from((d + 1) * TK) if d < KEY_STEPS - 1 else ())

    @pl.when(qi > 0)
    def _():
        add_corner(0, 0)

        def pair(i, carry):
            idx = first - 1 - 2 * i
            step(0, idx)
            step(1, idx - 1)
            return carry

        lax.fori_loop(0, first // 2 - 1, pair, 0)
        step(0, 1)
        step(1, 0, nxt=())

    accumulate(1, key_lo(0), all_groups)

    lv = lam_ref[...]
    lam = (jnp.exp(jnp.sum(lv[0:1] * lv[1:2], axis=-1, keepdims=True))
           - jnp.exp(jnp.sum(lv[2:3] * lv[3:4], axis=-1, keepdims=True)) + lambda_init)
    o = [acc_ref[c, 0:DIFF_VDIM, :] / acc_ref[c, DIFF_VDIM:DIFF_VDIM + 1, :] for c in range(2)]
    o = o[0] - lam * o[1]
    o = o * lax.rsqrt(jnp.mean(o * o, axis=0, keepdims=True) + EPS)
    o = o * sw_ref[...] * (1.0 - lambda_init)
    o_ref[...] = o.T.astype(BF16)


def _attention(qT, k, vT, corner, diag, meta, lamv, sw, *, batch, n_real, lp, lambda_init):
    nq = n_real // TQ
    return pl.pallas_call(
        functools.partial(_attn_kernel, n_real=n_real, lambda_init=lambda_init),
        grid=(batch, DIFF_HEADS, nq),
        in_specs=[
            pl.BlockSpec((1, DIFF_VDIM, TQ), lambda b, h, i: (b, h, i)),
            pl.BlockSpec((lp, DIFF_VDIM), lambda b, h, i: (b, h)),
            pl.BlockSpec((1, DIFF_VDIM, lp), lambda b, h, i: (b, h, 0)),
            pl.BlockSpec((1, BAND, BAND), lambda b, h, i: (h, 0, 0)),
            pl.BlockSpec((1, TK, TK), lambda b, h, i: (h, 0, 0)),
            pl.BlockSpec((1, 2, META_KEYS, BAND), lambda b, h, i: (h, 0, 0, 0)),
            _resident((4, DIFF_HEAD_DIM)),
            _resident((DIFF_VDIM, 1)),
        ],
        out_specs=pl.BlockSpec((TQ, DIFF_VDIM), lambda b, h, i: (b * nq + i, h)),
        out_shape=jax.ShapeDtypeStruct((batch * n_real, D_MODEL), BF16),
        scratch_shapes=[
            pltpu.VMEM((2, 2, TK, TQ), F32),
            pltpu.VMEM((2, 2, 1, TQ), F32),
            pltpu.VMEM((2, 2, TK, TQ), BF16),
            pltpu.VMEM((2, 2, 1, TQ), F32),
            pltpu.VMEM((2, 1, TQ), F32),
            pltpu.VMEM((2, V_ROWS, TQ), F32),
        ],
        compiler_params=_params("parallel", "parallel", "arbitrary"),
        name="diff_attention",
    )(qT, k, vT, corner, diag, meta, lamv, sw)


def kernel(x, meta_tokens, norm_w, ssm_in_w, ssm_conv_w, ssm_conv_b, ssm_dt_bias, ssm_a_log, ssm_d,
           ssm_norm_w, ssm_out_w, kv_norm_w, w_kv, w_q, lam_q1, lam_k1, lam_q2, lam_k2, subln_w,
           w_o, rel_bias, ffn_w_gu, ffn_w_down, final_norm_w):
    batch, n_real, d = x.shape
    assert d == D_MODEL and n_real % TQ == 0 and n_real % ROW_TILE == 0 and KEY_STEPS % 2 == 0
    assert norm_w.shape[0] == 2 and ssm_in_w.shape[0] == 1 and w_q.shape[0] == 1
    lp = n_real + CHUNK
    n_pad = CHUNK - N_META
    n_chunks = lp // CHUNK
    rows = batch * lp
    assert rows % ROW_TILE == 0
    row2 = lambda v: v.reshape(1, -1).astype(F32)

    tail = jnp.concatenate([jnp.zeros((n_pad, D_MODEL), F32), meta_tokens.astype(F32)], axis=0)

    in_w = ssm_in_w[0]
    w_zx = in_w[:, :D_INNER + CONV_DIM].astype(BF16)
    w_dt = jnp.pad(in_w[:, D_INNER + CONV_DIM:].astype(F32), ((0, 0), (0, LANES - SSM_HEADS)))
    w_dt_hi = w_dt.astype(BF16)
    w_dt_lo = (w_dt - w_dt_hi.astype(F32)).astype(BF16)
    z, xc, dt, dtT = _in_proj(x, tail, row2(norm_w[0, 0]), w_zx, w_dt_hi, w_dt_lo, ssm_conv_w[0].astype(F32),
                              row2(ssm_conv_b[0]), batch=batch, n_chunks=n_chunks)
    y = _ssd(xc, dt, dtT, row2(ssm_dt_bias[0]), ssm_dt_bias[0].reshape(-1, 1),
             row2(ssm_a_log[0]), ssm_a_log[0].reshape(-1, 1),
             row2(jnp.repeat(ssm_d[0], SSM_HEADDIM)), batch=batch, n_chunks=n_chunks, n_pad=n_pad)
    h2 = _ssd_out_ffn(y, z, row2(ssm_norm_w[0]), ssm_out_w[0].astype(BF16), x, tail, row2(norm_w[0, 1]),
                      ffn_w_gu[0].astype(BF16), ffn_w_down[0].astype(BF16), batch=batch, n_chunks=n_chunks)

    layer = 1
    lambda_init = 0.8 - 0.6 * math.exp(-0.3 * layer)
    k, vT, qT = _kvq(h2, row2(kv_norm_w), row2(norm_w[1, 0]), w_kv[:, :D_MODEL].astype(BF16),
                     w_kv[:, D_MODEL:].T.astype(BF16), w_q[0].T.astype(BF16), batch=batch, lp=lp)
    corner, diag, meta = _bias_tiles(rel_bias.astype(F32))
    lamv = jnp.stack([lam_q1[0], lam_k1[0], lam_q2[0], lam_k2[0]]).astype(F32)
    attn = _attention(qT, k, vT, corner, diag, meta, lamv, subln_w[0].reshape(-1, 1).astype(F32),
                      batch=batch, n_real=n_real, lp=lp, lambda_init=lambda_init)
    out = _attn_out_ffn(attn, w_o[0].astype(BF16), h2.reshape(batch, lp, D_MODEL), row2(norm_w[1, 1]),
                        ffn_w_gu[1].astype(BF16), ffn_w_down[1].astype(BF16),
                        row2(final_norm_w), batch=batch, n_real=n_real)
    return out.reshape(batch, n_real, D_MODEL)
```
